```python
import jax, jax.numpy as jnp
from jax import lax
import numpy as np

D_MODEL = 1024
BATCH = 8
SEQ = 4096
DEPTH = 2

CHUNK = 64
Q_BLOCK = 128

MLA_HEADS = 8
Q_LORA = 384
KV_LORA = 256
QK_NOPE = 64
QK_ROPE = 32
V_HEAD = 64
ROPE_THETA = 10000.0

MLSTM_HEADS = 8
MLSTM_HEAD_DIM = 64
MLSTM_W = MLSTM_HEADS * MLSTM_HEAD_DIM
MLSTM_CONV = 4

D_FF = 2816
FFN_CONV = 3

MLA_W = MLA_HEADS * V_HEAD
IN_SIZES = (Q_LORA, KV_LORA, QK_ROPE,
            MLSTM_W, MLSTM_W, MLSTM_W, MLSTM_W,
            MLSTM_HEADS, MLSTM_HEADS,
            D_MODEL, D_MODEL)
IN_COLS = sum(IN_SIZES)

kernel_name = "hybrid_mla_mlstm_convffn_adaln"


def _rmsnorm(x, w, eps=1e-6):
    x32 = x.astype(jnp.float32)
    y = x32 * lax.rsqrt(jnp.mean(x32 * x32, axis=-1, keepdims=True) + eps)
    return y.astype(x.dtype) * w


def _rope(x, positions):
    half = x.shape[-1] // 2
    inv = ROPE_THETA ** (-jnp.arange(half, dtype=jnp.float32) / half)
    ang = positions.astype(jnp.float32)[..., None] * inv
    if x.ndim == 4:
        ang = ang[:, :, None, :]
    cos = jnp.cos(ang).astype(x.dtype)
    sin = jnp.sin(ang).astype(x.dtype)
    x1, x2 = x[..., :half], x[..., half:]
    return jnp.concatenate([x1 * cos - x2 * sin, x2 * cos + x1 * sin], axis=-1)


def _causal_dwconv(x, w, b):
    k_w, ch = w.shape
    y = lax.conv_general_dilated(
        x, w[:, None, :].astype(x.dtype), window_strides=(1,),
        padding=[(k_w - 1, 0)], dimension_numbers=("NWC", "WIO", "NWC"),
        feature_group_count=ch)
    return y + b


def _mla_attention(q_nope, q_rope, k_nope, k_rope, v):
    s_len = q_nope.shape[1]
    scale = (QK_NOPE + QK_ROPE) ** -0.5
    outs = []
    for qb in range(s_len // Q_BLOCK):
        s0, e = qb * Q_BLOCK, (qb + 1) * Q_BLOCK
        sc = (jnp.einsum("bqhd,bkhd->bhqk", q_nope[:, s0:e], k_nope[:, :e])
              + jnp.einsum("bqhr,bkr->bhqk", q_rope[:, s0:e], k_rope[:, :e]))
        sc = sc.astype(jnp.float32) * scale
        q_chunk = (s0 + jnp.arange(Q_BLOCK)) // CHUNK
        k_chunk = jnp.arange(e) // CHUNK
        sc = jnp.where(k_chunk[None, :] <= q_chunk[:, None], sc, -jnp.inf)
        p = jax.nn.softmax(sc, axis=-1).astype(v.dtype)
        outs.append(jnp.einsum("bhqk,bkhd->bqhd", p, v[:, :e]))
    return jnp.concatenate(outs, axis=1)


def _mlstm_chunkwise(q, k, v, i_pre, log_f):
    bsz, s_len, nh, dk = q.shape
    dv = v.shape[-1]
    L = CHUNK
    nc = s_len // L
    f32 = jnp.float32

    def to_chunks(a):
        return jnp.moveaxis(a.astype(f32).reshape((bsz, nc, L) + a.shape[2:]), 1, 0)

    xs = tuple(to_chunks(a) for a in (q, k, v, i_pre, log_f))
    causal = jnp.tril(jnp.ones((L, L), dtype=bool))

    def step(carry, inp):
        C, n, m = carry
        qc, kc, vc, ic, fc = inp
        b = jnp.cumsum(fc, axis=1).transpose(0, 2, 1)
        ih = ic.transpose(0, 2, 1)
        log_d = jnp.where(causal, b[..., :, None] - b[..., None, :] + ih[..., None, :], -jnp.inf)
        log_inter = b + m[..., None]
        m_t = jnp.maximum(log_inter, jnp.max(log_d, axis=-1))
        w_intra = jnp.exp(log_d - m_t[..., None])
        w_inter = jnp.exp(log_inter - m_t)
        s = jnp.einsum("bthd,bshd->bhts", qc, kc) * w_intra
        num = (jnp.einsum("bhts,bshv->bhtv", s, vc)
               + w_inter[..., None] * jnp.einsum("bhvk,bthk->bhtv", C, qc))
        den = s.sum(-1) + w_inter * jnp.einsum("bhk,bthk->bht", n, qc)
        h = num / jnp.maximum(jnp.abs(den), jnp.exp(-m_t))[..., None]
        b_last = b[..., -1]
        log_w = b_last[..., None] - b + ih
        m_new = jnp.maximum(b_last + m, jnp.max(log_w, axis=-1))
        w_s = jnp.exp(log_w - m_new[..., None])
        decay = jnp.exp(b_last + m - m_new)
        C = decay[..., None, None] * C + jnp.einsum("bhs,bshv,bshk->bhvk", w_s, vc, kc)
        n = decay[..., None] * n + jnp.einsum("bhs,bshk->bhk", w_s, kc)
        return (C, n, m_new), h.transpose(0, 2, 1, 3)

    init = (jnp.zeros((bsz, nh, dv, dk), f32), jnp.zeros((bsz, nh, dk), f32),
            jnp.zeros((bsz, nh), f32))
    _, hs = lax.scan(step, init, xs)
    return jnp.moveaxis(hs, 0, 1).reshape(bsz, s_len, nh, dv).astype(v.dtype)


def _mixer(h, positions, w_in, q_norm_w, kv_norm_w, w_uq, w_ukv, conv_w, conv_b,
           gate_b, head_norm_w, w_br_mla, w_br_mlstm, w_out):
    bsz, s_len, _ = h.shape
    split_at = [int(s) for s in np.cumsum(IN_SIZES)[:-1]]
    proj = h @ w_in
    c_q, c_kv, k_rope, q_m, k_m, v_m, o_m, i_m, f_m, g_mla, g_mlstm = jnp.split(proj, split_at, axis=-1)

    q = (_rmsnorm(c_q, q_norm_w) @ w_uq).reshape(bsz, s_len, MLA_HEADS, QK_NOPE + QK_ROPE)
    q_nope, q_rope = q[..., :QK_NOPE], _rope(q[..., QK_NOPE:], positions)
    kv = (_rmsnorm(c_kv, kv_norm_w) @ w_ukv).reshape(bsz, s_len, MLA_HEADS, QK_NOPE + V_HEAD)
    k_nope, v = kv[..., :QK_NOPE], kv[..., QK_NOPE:]
    k_rope = _rope(k_rope, positions)
    y_mla = _mla_attention(q_nope, q_rope, k_nope, k_rope, v).reshape(bsz, s_len, MLA_W)

    qk = jax.nn.silu(_causal_dwconv(jnp.concatenate([q_m, k_m], axis=-1), conv_w, conv_b))
    q_m, k_m = jnp.split(qk, 2, axis=-1)
    i_pre, f_pre = jnp.split(jnp.concatenate([i_m, f_m], axis=-1) + gate_b, 2, axis=-1)
    hd = (bsz, s_len, MLSTM_HEADS, MLSTM_HEAD_DIM)
    hm = _mlstm_chunkwise(q_m.reshape(hd), k_m.reshape(hd) * (MLSTM_HEAD_DIM ** -0.5),
                          v_m.reshape(hd), i_pre.astype(jnp.float32),
                          jax.nn.log_sigmoid(f_pre.astype(jnp.float32)))
    hm = _rmsnorm(hm, head_norm_w.reshape(MLSTM_HEADS, MLSTM_HEAD_DIM))
    y_mlstm = jax.nn.sigmoid(o_m) * hm.reshape(bsz, s_len, MLSTM_W)

    merged = (jax.nn.sigmoid(g_mla) * (y_mla @ w_br_mla)
              + jax.nn.sigmoid(g_mlstm) * (y_mlstm @ w_br_mlstm))
    return merged @ w_out


def _conv_ffn(h, w_up, conv_w, conv_b, w_down):
    u = _causal_dwconv(h @ w_up, conv_w, conv_b)
    a, val = jnp.split(u, 2, axis=-1)
    return (jax.nn.gelu(a) * val) @ w_down


def setup_inputs(seed: int = 0) -> dict:
    key = jax.random.key(seed)
    ks = jax.random.split(key, 32)
    f32 = jnp.float32

    def nrm(k, shape, scale):
        return jax.random.normal(k, shape, f32) * scale

    def gain(k, shape):
        return 1.0 + 0.02 * jax.random.normal(k, shape, f32)

    offsets = jax.random.randint(ks[2], (BATCH, 1), 0, 4096, dtype=jnp.int32)
    positions = offsets + jnp.arange(SEQ, dtype=jnp.int32)[None, :]
    gate_b = jnp.concatenate([
        0.1 * jax.random.normal(ks[12], (DEPTH, MLSTM_HEADS), f32),
        3.0 + 3.0 * jax.random.uniform(ks[13], (DEPTH, MLSTM_HEADS), f32)], axis=-1)
    return {
        "x": nrm(ks[0], (BATCH, SEQ, D_MODEL), 1.0),
        "c": nrm(ks[1], (BATCH, D_MODEL), 1.0),
        "positions": positions,
        "ada_w": nrm(ks[3], (DEPTH, D_MODEL, 6 * D_MODEL), 0.5 * D_MODEL ** -0.5),
        "ada_b": nrm(ks[4], (DEPTH, 6 * D_MODEL), 0.02),
        "norm_mix_w": gain(ks[5], (DEPTH, D_MODEL)),
        "w_in": nrm(ks[6], (DEPTH, D_MODEL, IN_COLS), D_MODEL ** -0.5),
        "q_norm_w": gain(ks[7], (DEPTH, Q_LORA)),
        "kv_norm_w": gain(ks[8], (DEPTH, KV_LORA)),
        "w_uq": nrm(ks[9], (DEPTH, Q_LORA, MLA_HEADS * (QK_NOPE + QK_ROPE)), Q_LORA ** -0.5),
        "w_ukv": nrm(ks[10], (DEPTH, KV_LORA, MLA_HEADS * (QK_NOPE + V_HEAD)), KV_LORA ** -0.5),
        "mlstm_conv_w": nrm(ks[11], (DEPTH, MLSTM_CONV, 2 * MLSTM_W), MLSTM_CONV ** -0.5),
        "mlstm_conv_b": nrm(ks[14], (DEPTH, 2 * MLSTM_W), 0.02),
        "mlstm_gate_b": gate_b,
        "mlstm_head_norm_w": gain(ks[15], (DEPTH, MLSTM_W)),
        "w_br_mla": nrm(ks[16], (DEPTH, MLA_W, D_MODEL), MLA_W ** -0.5),
        "w_br_mlstm": nrm(ks[17], (DEPTH, MLSTM_W, D_MODEL), MLSTM_W ** -0.5),
        "w_out": nrm(ks[18], (DEPTH, D_MODEL, D_MODEL), D_MODEL ** -0.5),
        "norm_ffn_w": gain(ks[19], (DEPTH, D_MODEL)),
        "ffn_w_up": nrm(ks[20], (DEPTH, D_MODEL, 2 * D_FF), D_MODEL ** -0.5),
        "ffn_conv_w": nrm(ks[21], (DEPTH, FFN_CONV, 2 * D_FF), FFN_CONV ** -0.5),
        "ffn_conv_b": nrm(ks[22], (DEPTH, 2 * D_FF), 0.02),
        "ffn_w_down": nrm(ks[23], (DEPTH, D_FF, D_MODEL), D_FF ** -0.5),
        "final_norm_w": gain(ks[24], (D_MODEL,)),
    }


def reference(x, c, positions, ada_w, ada_b, norm_mix_w, w_in, q_norm_w, kv_norm_w,
              w_uq, w_ukv, mlstm_conv_w, mlstm_conv_b, mlstm_gate_b, mlstm_head_norm_w,
              w_br_mla, w_br_mlstm, w_out, norm_ffn_w, ffn_w_up, ffn_conv_w, ffn_conv_b,
              ffn_w_down, final_norm_w):
    c_act = jax.nn.silu(c)
    for l in range(DEPTH):
        mod = (c_act @ ada_w[l] + ada_b[l])[:, None, :]
        sh1, sc1, g1, sh2, sc2, g2 = jnp.split(mod, 6, axis=-1)
        h = _rmsnorm(x, norm_mix_w[l]) * (1 + sc1) + sh1
        x = x + g1 * _mixer(h, positions, w_in[l], q_norm_w[l], kv_norm_w[l], w_uq[l], w_ukv[l],
                            mlstm_conv_w[l], mlstm_conv_b[l], mlstm_gate_b[l],
                            mlstm_head_norm_w[l], w_br_mla[l], w_br_mlstm[l], w_out[l])
        h = _rmsnorm(x, norm_ffn_w[l]) * (1 + sc2) + sh2
        x = x + g2 * _conv_ffn(h, ffn_w_up[l], ffn_conv_w[l], ffn_conv_b[l], ffn_w_down[l])
    return _rmsnorm(x, final_norm_w)
```

```python
import functools

import jax
import jax.numpy as jnp
from jax import lax
from jax.experimental import pallas as pl
from jax.experimental.pallas import tpu as pltpu

F32 = jnp.float32
BF16 = jnp.bfloat16

D_MODEL = 1024
CHUNK = 64
MLA_HEADS = 8
Q_LORA = 384
KV_LORA = 256
QK_NOPE = 64
QK_ROPE = 32
V_HEAD = 64
ROPE_THETA = 10000.0
MLSTM_HEADS = 8
MLSTM_HEAD_DIM = 64
MLSTM_W = MLSTM_HEADS * MLSTM_HEAD_DIM
MLSTM_CONV = 4
D_FF = 2816
FFN_CONV = 3
MLA_W = MLA_HEADS * V_HEAD
EPS = 1e-6

LANES = 128
SUBLANES = 8
HEAD_SLAB = 128
PAIR = 2 * MLSTM_HEAD_DIM
NEG = -1e30

A_CQ = 0
A_CKV = A_CQ + Q_LORA
A_KR = A_CKV + KV_LORA
A_KRR = A_KR + LANES
A_QK = A_KRR + LANES
A_V = A_QK + 2 * MLSTM_W
A_G = A_V + MLSTM_W
A_COLS = A_G + LANES
C_O = 0
C_GA = C_O + MLSTM_W
C_GM = C_GA + D_MODEL
C_COLS = C_GM + D_MODEL

FF_CHUNK = 256
N_FF = D_FF // FF_CHUNK

VMEM_LIMIT = 56 * 1024 * 1024


def _params(n_axes):
    return pltpu.CompilerParams(dimension_semantics=("arbitrary",) * n_axes,
                                vmem_limit_bytes=VMEM_LIMIT)


def _resident(shape):
    nd = len(shape)
    return pl.BlockSpec(shape, lambda *_: (0,) * nd, pipeline_mode=pl.Buffered(1))


def _rms(x):
    return x * lax.rsqrt(jnp.mean(x * x, axis=-1, keepdims=True) + EPS)


def _dot(a, b):
    return jnp.dot(a, b, preferred_element_type=F32)


def _dot_nt(a, b):
    return lax.dot_general(a, b, (((1,), (1,)), ((), ())), preferred_element_type=F32)


def _mod_kernel(c_ref, w_ref, b_ref, o_ref):
    c = c_ref[...]
    c_act = (c * jax.nn.sigmoid(c)).astype(BF16)
    o_ref[0] = _dot(c_act, w_ref[0].astype(BF16)) + b_ref[0]


def _modulation(c, ada_w, ada_b):
    depth, d, n = ada_w.shape
    bsz = c.shape[0]
    tn = 1024
    return pl.pallas_call(
        _mod_kernel,
        grid=(depth, n // tn),
        in_specs=[pl.BlockSpec((bsz, d), lambda l, j: (0, 0)),
                  pl.BlockSpec((1, d, tn), lambda l, j: (l, 0, j)),
                  pl.BlockSpec((1, 1, tn), lambda l, j: (l, 0, j))],
        out_specs=pl.BlockSpec((1, bsz, tn), lambda l, j: (l, 0, j)),
        out_shape=jax.ShapeDtypeStruct((depth, bsz, n), F32),
        compiler_params=_params(2),
        name="modulation",
    )(c, ada_w, ada_b.reshape(depth, 1, n))


def _rope_kernel(pos_ref, inv_ref, cos_ref, sin_ref):
    ang = pos_ref[...].astype(F32) * inv_ref[...]
    cos_ref[...] = jnp.cos(ang)
    sin_ref[...] = jnp.sin(ang)


def _rope_tables(positions):
    bsz, s_len = positions.shape
    half = QK_ROPE // 2
    per_row = LANES // half
    rows = bsz * s_len // per_row
    pos = jnp.repeat(positions.reshape(rows, per_row), half, axis=1)
    inv = ROPE_THETA ** (-jnp.arange(half, dtype=F32) / half)
    inv = jnp.tile(inv, per_row).reshape(1, LANES)
    tr = min(rows, 512)
    cos, sin = pl.pallas_call(
        _rope_kernel,
        grid=(rows // tr,),
        in_specs=[pl.BlockSpec((tr, LANES), lambda i: (i, 0)),
                  pl.BlockSpec((1, LANES), lambda i: (0, 0))],
        out_specs=[pl.BlockSpec((tr, LANES), lambda i: (i, 0))] * 2,
        out_shape=[jax.ShapeDtypeStruct((rows, LANES), F32)] * 2,
        compiler_params=_params(1),
        name="rope_tables",
    )(pos, inv)
    return cos.reshape(bsz, s_len, half), sin.reshape(bsz, s_len, half)


def _in_proj_kernel(x_ref, mod_ref, nw_ref, win_ref, qnw_ref, kvnw_ref, wuqa_ref, wuqb_ref,
                    wuk_ref, wuv_ref, cw_ref, cb_ref, gb_ref, cos_ref, sin_ref,
                    q_ref, k_ref, v_ref, qm_ref, km_ref, vm_ref, g_ref, cbuf, *, tm):
    halo = SUBLANES
    mod = mod_ref[0]
    h = _rms(x_ref[0]) * nw_ref[...] * (1.0 + mod[1:2]) + mod[0:1]
    hb = h.astype(BF16)
    cos = cos_ref[0]
    sin = sin_ref[0]

    def proj(lo, width):
        return _dot(hb, win_ref[:, lo:lo + width])

    cqn = (_rms(proj(A_CQ, Q_LORA)) * qnw_ref[...]).astype(BF16)
    qa = _dot(cqn, wuqa_ref[...])
    qb = _dot(cqn, wuqb_ref[...])
    scale = (QK_NOPE + QK_ROPE) ** -0.5
    for hd in range(MLA_HEADS):
        sl = slice(hd * HEAD_SLAB, (hd + 1) * HEAD_SLAB)
        q_ref[0, :, sl] = ((qa[:, sl] * cos + qb[:, sl] * sin) * scale).astype(BF16)

    ckvn = (_rms(proj(A_CKV, KV_LORA)) * kvnw_ref[...]).astype(BF16)
    kn = _dot(ckvn, wuk_ref[...])
    kr = proj(A_KR, LANES) * cos + proj(A_KRR, LANES) * sin
    for hd in range(MLA_HEADS):
        sl = slice(hd * HEAD_SLAB, (hd + 1) * HEAD_SLAB)
        k_ref[0, :, sl] = (kn[:, sl] + kr).astype(BF16)
    v_ref[0] = _dot(ckvn, wuv_ref[...]).astype(BF16)

    @pl.when(pl.program_id(1) == 0)
    def _():
        cbuf[0:halo, :] = jnp.zeros((halo, 2 * MLSTM_W), F32)

    cbuf[halo:halo + tm, :] = proj(A_QK, 2 * MLSTM_W)
    y = cb_ref[...] + cw_ref[3:4, :] * cbuf[halo:halo + tm, :]
    for j in range(1, MLSTM_CONV):
        y = y + cw_ref[3 - j:4 - j, :] * cbuf[halo - j:halo - j + tm, :]
    cbuf[0:halo, :] = cbuf[tm:tm + halo, :]
    y = y * jax.nn.sigmoid(y)
    qm_ref[0] = y[:, :MLSTM_W].astype(BF16)
    km_ref[0] = (y[:, MLSTM_W:] * (MLSTM_HEAD_DIM ** -0.5)).astype(BF16)
    vm_ref[0] = proj(A_V, MLSTM_W).astype(BF16)

    g = proj(A_G, LANES) + gb_ref[...]
    logf = jnp.minimum(g, 0.0) - jnp.log1p(jnp.exp(-jnp.abs(g)))
    lane = lax.broadcasted_iota(jnp.int32, g.shape, 1)
    g_ref[0] = jnp.where(lane < MLSTM_HEADS, g, logf)


def _in_proj(x, mod, nw, w_a, qnw, kvnw, wuqa, wuqb, wuk, wuv, cw, cb, gb, cosp, sinp, *, tm):
    bsz, s_len, d = x.shape
    row = lambda w: pl.BlockSpec((1, tm, w), lambda b, s: (b, s, 0))
    out_widths = (MLA_HEADS * HEAD_SLAB, MLA_HEADS * HEAD_SLAB, MLA_W, MLSTM_W, MLSTM_W, MLSTM_W)
    out_shape = [jax.ShapeDtypeStruct((bsz, s_len, w), BF16) for w in out_widths]
    out_shape.append(jax.ShapeDtypeStruct((bsz, s_len, LANES), F32))
    return pl.pallas_call(
        functools.partial(_in_proj_kernel, tm=tm),
        grid=(bsz, s_len // tm),
        in_specs=[row(d),
                  pl.BlockSpec((1, 6, d), lambda b, s: (b, 0, 0)),
                  _resident(nw.shape), _resident(w_a.shape), _resident(qnw.shape),
                  _resident(kvnw.shape), _resident(wuqa.shape), _resident(wuqb.shape),
                  _resident(wuk.shape), _resident(wuv.shape), _resident(cw.shape),
                  _resident(cb.shape), _resident(gb.shape), row(LANES), row(LANES)],
        out_specs=[row(w) for w in out_widths] + [row(LANES)],
        out_shape=out_shape,
        scratch_shapes=[pltpu.VMEM((tm + 2 * SUBLANES, 2 * MLSTM_W), F32)],
        compiler_params=_params(2),
        name="in_proj",
    )(x, mod, nw, w_a, qnw, kvnw, wuqa, wuqb, wuk, wuv, cw, cb, gb, cosp, sinp)


def _attn_kernel(q_ref, k_ref, v_ref, o_ref, *, tq):
    qi = pl.program_id(2)
    row_chunk = lax.broadcasted_iota(jnp.int32, (tq, tq), 0) // CHUNK
    col_chunk = lax.broadcasted_iota(jnp.int32, (tq, tq), 1) // CHUNK
    visible = col_chunk <= row_chunk
    outs = []
    for hh in range(2):
        sl = slice(hh * HEAD_SLAB, (hh + 1) * HEAD_SLAB)
        q = q_ref[0, :, sl]

        def step(j, carry, masked):
            m, l, acc = carry
            start = pl.multiple_of(j * tq, tq)
            s = _dot_nt(q, k_ref[0, pl.ds(start, tq), sl])
            if masked:
                s = jnp.where(visible, s, NEG)
            m_new = jnp.maximum(m, jnp.max(s, axis=-1, keepdims=True))
            alpha = jnp.exp(m - m_new)
            p = jnp.exp(s - m_new)
            l = alpha * l + jnp.sum(p, axis=-1, keepdims=True)
            acc = alpha * acc + _dot(p.astype(BF16), v_ref[0, pl.ds(start, tq), :])
            return m_new, l, acc

        init = (jnp.full((tq, 1), NEG, F32), jnp.zeros((tq, 1), F32), jnp.zeros((tq, PAIR), F32))
        carry = step(qi, init, True)
        m, l, acc = lax.fori_loop(0, qi, functools.partial(step, masked=False), carry)
        outs.append(acc / l)
    lane = lax.broadcasted_iota(jnp.int32, (tq, PAIR), 1)
    o_ref[0] = jnp.where(lane < V_HEAD, outs[0], outs[1]).astype(BF16)


def _attention(q, k, v, *, tq):
    bsz, s_len, _ = q.shape
    return pl.pallas_call(
        functools.partial(_attn_kernel, tq=tq),
        grid=(bsz, MLA_HEADS // 2, s_len // tq),
        in_specs=[pl.BlockSpec((1, tq, 2 * HEAD_SLAB), lambda b, h, i: (b, i, h)),
                  pl.BlockSpec((1, s_len, 2 * HEAD_SLAB), lambda b, h, i: (b, 0, h)),
                  pl.BlockSpec((1, s_len, PAIR), lambda b, h, i: (b, 0, h))],
        out_specs=pl.BlockSpec((1, tq, PAIR), lambda b, h, i: (b, i, h)),
        out_shape=jax.ShapeDtypeStruct((bsz, s_len, MLA_W), BF16),
        compiler_params=_params(3),
        name="mla_attention",
    )(q, k, v)


def _split3(a):
    hi = a.astype(BF16)
    r = a - hi.astype(F32)
    mid = r.astype(BF16)
    lo = (r - mid.astype(F32)).astype(BF16)
    return hi, mid, lo


def _mlstm_kernel(q_ref, k_ref, v_ref, g_ref, nw_ref, o_ref, c_st, n_st, m_st, *, tl):
    @pl.when(pl.program_id(1) == 0)
    def _():
        c_st[...] = jnp.zeros(c_st.shape, F32)
        n_st[...] = jnp.zeros(n_st.shape, F32)
        m_st[...] = jnp.zeros(m_st.shape, F32)

    nh = MLSTM_HEADS
    g = g_ref[0]
    row = lax.broadcasted_iota(jnp.int32, (tl, tl), 0)
    col = lax.broadcasted_iota(jnp.int32, (tl, tl), 1)
    causal = col <= row
    tri = jnp.where(causal, 1.0, 0.0).astype(BF16)
    b_all = sum(_dot(tri, part) for part in _split3(g))
    g_t = g.T
    b_t = b_all.T
    lane = lax.broadcasted_iota(jnp.int32, (tl, PAIR), 1)
    first = lane < MLSTM_HEAD_DIM
    lane_row = lax.broadcasted_iota(jnp.int32, (1, PAIR), 1) < MLSTM_HEAD_DIM
    blk = (lax.broadcasted_iota(jnp.int32, (PAIR, PAIR), 0) < MLSTM_HEAD_DIM) == \
          (lax.broadcasted_iota(jnp.int32, (PAIR, PAIR), 1) < MLSTM_HEAD_DIM)

    def pick(c0, c1):
        return jnp.where(first, c0, c1)

    for p in range(nh // 2):
        sl = slice(p * PAIR, (p + 1) * PAIR)
        qp = q_ref[0, :, sl]
        kp = k_ref[0, :, sl]
        vp = v_ref[0, :, sl]
        qf = qp.astype(F32)
        c_old = c_st[p]
        n_old = n_st[p]
        inter = _dot(qp, c_old.astype(BF16))
        nums, dens, w_inters, e_negs, w_ks, decays, m_news = [], [], [], [], [], [], []
        for hh in range(2):
            hd = 2 * p + hh
            b_col = b_all[:, nh + hd:nh + hd + 1]
            a_row = g_t[hd:hd + 1, :] - b_t[nh + hd:nh + hd + 1, :]
            a_col = g[:, hd:hd + 1] - b_col
            m_prev = m_st[p, :, hh * MLSTM_HEAD_DIM:hh * MLSTM_HEAD_DIM + 1]
            log_d = jnp.where(causal, b_col + a_row, NEG)
            log_inter = b_col + m_prev
            m_t = jnp.maximum(log_inter, jnp.max(log_d, axis=-1, keepdims=True))
            w_intra = jnp.exp(log_d - m_t)
            w_inter = jnp.exp(log_inter - m_t)
            q_h = jnp.where(first == (hh == 0), qp, jnp.zeros_like(qp))
            s = _dot_nt(q_h, kp) * w_intra
            nums.append(_dot(s.astype(BF16), vp))
            qn = jnp.sum(jnp.where(first == (hh == 0), qf * n_old, 0.0), axis=-1, keepdims=True)
            dens.append(jnp.sum(s, axis=-1, keepdims=True) + w_inter * qn)
            w_inters.append(w_inter)
            e_negs.append(jnp.exp(-m_t))
            b_last = b_col[tl - 1:tl, :]
            log_w = b_last + a_col
            m_new = jnp.maximum(b_last + m_prev, jnp.max(log_w, axis=0, keepdims=True))
            w_ks.append(jnp.exp(log_w - m_new))
            decays.append(jnp.exp(b_last + m_prev - m_new))
            m_news.append(m_new)
        num = pick(nums[0], nums[1]) + pick(w_inters[0], w_inters[1]) * inter
        den = jnp.maximum(jnp.abs(pick(dens[0], dens[1])), pick(e_negs[0], e_negs[1]))
        hp = num / den
        sq = hp * hp
        ms0 = jnp.sum(jnp.where(first, sq, 0.0), axis=-1, keepdims=True)
        ms1 = jnp.sum(jnp.where(first, 0.0, sq), axis=-1, keepdims=True)
        ms = pick(ms0, ms1) * (1.0 / MLSTM_HEAD_DIM)
        o_ref[0, :, sl] = hp * lax.rsqrt(ms + EPS) * nw_ref[:, sl]
        kw = kp.astype(F32) * pick(w_ks[0], w_ks[1])
        dec = jnp.where(lane_row, decays[0], decays[1])
        upd = _dot(kw.T.astype(BF16), vp)
        c_st[p] = dec * c_old + jnp.where(blk, upd, 0.0)
        n_st[p] = dec * n_old + jnp.sum(kw, axis=0, keepdims=True)
        m_st[p] = jnp.where(lane_row, m_news[0], m_news[1])


def _mlstm(qm, km, vm, gates, head_nw, *, tl):
    bsz, s_len, w = qm.shape
    row = lambda width: pl.BlockSpec((1, tl, width), lambda b, s: (b, s, 0))
    n_pairs = MLSTM_HEADS // 2
    return pl.pallas_call(
        functools.partial(_mlstm_kernel, tl=tl),
        grid=(bsz, s_len // tl),
        in_specs=[row(w), row(w), row(w), row(LANES), _resident(head_nw.shape)],
        out_specs=row(w),
        out_shape=jax.ShapeDtypeStruct((bsz, s_len, w), F32),
        scratch_shapes=[pltpu.VMEM((n_pairs, PAIR, PAIR), F32),
                        pltpu.VMEM((n_pairs, 1, PAIR), F32),
                        pltpu.VMEM((n_pairs, 1, PAIR), F32)],
        compiler_params=_params(2),
        name="mlstm",
    )(qm, km, vm, gates, head_nw)


def _mix_out_kernel(x_ref, mod_ref, nw_ref, wc_ref, ya_ref, hm_ref, wba_ref, wbm_ref, wo_ref,
                    o_ref):
    mod = mod_ref[0]
    x = x_ref[0]
    hb = (_rms(x) * nw_ref[...] * (1.0 + mod[1:2]) + mod[0:1]).astype(BF16)
    o_gate = jax.nn.sigmoid(_dot(hb, wc_ref[:, C_O:C_O + MLSTM_W]))
    y_mlstm = (o_gate * hm_ref[0]).astype(BF16)
    merged = jax.nn.sigmoid(_dot(hb, wc_ref[:, C_GA:C_GA + D_MODEL])) * _dot(ya_ref[0], wba_ref[...])
    merged = merged + (jax.nn.sigmoid(_dot(hb, wc_ref[:, C_GM:C_GM + D_MODEL]))
                       * _dot(y_mlstm, wbm_ref[...]))
    o_ref[0] = x + mod[2:3] * _dot(merged.astype(BF16), wo_ref[...])


def _mix_out(x, mod, nw, w_c, y_mla, hm, wba, wbm, wo, *, tm):
    bsz, s_len, d = x.shape
    row = lambda w: pl.BlockSpec((1, tm, w), lambda b, s: (b, s, 0))
    return pl.pallas_call(
        _mix_out_kernel,
        grid=(bsz, s_len // tm),
        in_specs=[row(d), pl.BlockSpec((1, 6, d), lambda b, s: (b, 0, 0)),
                  _resident(nw.shape), _resident(w_c.shape), row(MLA_W), row(MLSTM_W),
                  _resident(wba.shape), _resident(wbm.shape), _resident(wo.shape)],
        out_specs=row(d),
        out_shape=jax.ShapeDtypeStruct((bsz, s_len, d), F32),
        compiler_params=_params(2),
        name="mix_out",
    )(x, mod, nw, w_c, y_mla, hm, wba, wbm, wo)


def _gelu_tanh(a):
    return 0.5 * a * (1.0 + jnp.tanh(0.7978845608028654 * (a + 0.044715 * (a * a * a))))


def _ffn_kernel(x_ref, mod_ref, nw_ref, wup_ref, cw_ref, cb_ref, wdn_ref, fnw_ref, o_ref,
                ubuf, halo_ref, acc_ref, *, tm, final_norm):
    halo = SUBLANES
    mod = mod_ref[0]
    x = x_ref[0]
    hb = (_rms(x) * nw_ref[...] * (1.0 + mod[4:5]) + mod[3:4]).astype(BF16)

    @pl.when(pl.program_id(1) == 0)
    def _():
        halo_ref[...] = jnp.zeros(halo_ref.shape, F32)

    for c in range(N_FF):
        ubuf[0:halo, :] = halo_ref[c]
        ubuf[halo:halo + tm, :] = _dot(hb, wup_ref[c])
        halo_ref[c] = ubuf[tm:tm + halo, :]
        u = cb_ref[c] + cw_ref[c, 2:3, :] * ubuf[halo:halo + tm, :]
        for j in range(1, FFN_CONV):
            u = u + cw_ref[c, 2 - j:3 - j, :] * ubuf[halo - j:halo - j + tm, :]
        gated = (_gelu_tanh(u[:, :FF_CHUNK]) * u[:, FF_CHUNK:]).astype(BF16)
        part = _dot(gated, wdn_ref[c])
        if c == 0:
            acc_ref[...] = part
        else:
            acc_ref[...] += part
    y = x + mod[5:6] * acc_ref[...]
    if final_norm:
        y = _rms(y) * fnw_ref[...]
    o_ref[0] = y


def _ffn(x, mod, nw, wup, cw, cb, wdn, fnw, *, tm, final_norm):
    bsz, s_len, d = x.shape
    row = lambda w: pl.BlockSpec((1, tm, w), lambda b, s: (b, s, 0))
    return pl.pallas_call(
        functools.partial(_ffn_kernel, tm=tm, final_norm=final_norm),
        grid=(bsz, s_len // tm),
        in_specs=[row(d), pl.BlockSpec((1, 6, d), lambda b, s: (b, 0, 0)),
                  _resident(nw.shape), _resident(wup.shape), _resident(cw.shape),
                  _resident(cb.shape), _resident(wdn.shape), _resident(fnw.shape)],
        out_specs=row(d),
        out_shape=jax.ShapeDtypeStruct((bsz, s_len, d), F32),
        scratch_shapes=[pltpu.VMEM((tm + 2 * SUBLANES, 2 * FF_CHUNK), F32),
                        pltpu.VMEM((N_FF, SUBLANES, 2 * FF_CHUNK), F32),
                        pltpu.VMEM((tm, d), F32)],
        compiler_params=_params(2),
        name="conv_ffn",
    )(x, mod, nw, wup, cw, cb, wdn, fnw)


def _rot_cols(w):
    half = w.shape[-1] // 2
    return jnp.concatenate([-w[..., half:], w[..., :half]], axis=-1)


def _pad_cols(w, before, total):
    pad = [(0, 0)] * (w.ndim - 1) + [(before, total - before - w.shape[-1])]
    return jnp.pad(w, pad)


def _layer_weights(w_in, w_uq, w_ukv, gate_b):
    d = w_in.shape[0]
    o = 0
    seg = {}
    for name, width in (("cq", Q_LORA), ("ckv", KV_LORA), ("kr", QK_ROPE), ("qm", MLSTM_W),
                        ("km", MLSTM_W), ("vm", MLSTM_W), ("om", MLSTM_W), ("im", MLSTM_HEADS),
                        ("fm", MLSTM_HEADS), ("ga", D_MODEL), ("gm", D_MODEL)):
        seg[name] = w_in[:, o:o + width]
        o += width
    gates = _pad_cols(jnp.concatenate([seg["im"], seg["fm"]], axis=1), 0, LANES)
    w_a = jnp.concatenate([seg["cq"], seg["ckv"],
                           _pad_cols(seg["kr"], QK_NOPE, LANES),
                           _pad_cols(_rot_cols(seg["kr"]), QK_NOPE, LANES),
                           seg["qm"], seg["km"], seg["vm"], gates], axis=1).astype(BF16)
    w_c = jnp.concatenate([seg["om"], seg["ga"], seg["gm"]], axis=1).astype(BF16)
    uq = w_uq.reshape(Q_LORA, MLA_HEADS, QK_NOPE + QK_ROPE)
    wuqa = _pad_cols(uq, 0, HEAD_SLAB).reshape(Q_LORA, -1).astype(BF16)
    wuqb = _pad_cols(_rot_cols(uq[..., QK_NOPE:]), QK_NOPE, HEAD_SLAB).reshape(Q_LORA, -1).astype(BF16)
    ukv = w_ukv.reshape(KV_LORA, MLA_HEADS, QK_NOPE + V_HEAD)
    wuk = _pad_cols(ukv[..., :QK_NOPE], 0, HEAD_SLAB).reshape(KV_LORA, -1).astype(BF16)
    wuv = ukv[..., QK_NOPE:].reshape(KV_LORA, -1).astype(BF16)
    gb = _pad_cols(gate_b.reshape(1, -1), 0, LANES)
    assert w_a.shape == (d, A_COLS) and w_c.shape == (d, C_COLS)
    return w_a, w_c, wuqa, wuqb, wuk, wuv, gb


def _ffn_weights(w_up, conv_w, conv_b, w_down):
    def chunked(a):
        lead = a.shape[:-1]
        a = a.reshape(lead + (2, N_FF, FF_CHUNK))
        a = jnp.moveaxis(a, -2, 0)
        return a.reshape((N_FF,) + lead + (2 * FF_CHUNK,))
    return (chunked(w_up).astype(BF16), chunked(conv_w), chunked(conv_b.reshape(1, -1)),
            w_down.reshape(N_FF, FF_CHUNK, -1).astype(BF16))


def kernel(x, c, positions, ada_w, ada_b, norm_mix_w, w_in, q_norm_w, kv_norm_w, w_uq, w_ukv,
           mlstm_conv_w, mlstm_conv_b, mlstm_gate_b, mlstm_head_norm_w, w_br_mla, w_br_mlstm,
           w_out, norm_ffn_w, ffn_w_up, ffn_conv_w, ffn_conv_b, ffn_w_down, final_norm_w):
    bsz, s_len, d = x.shape
    depth = ada_w.shape[0]
    tm = min(512, s_len)
    tq = min(256, s_len)
    tl = min(256, s_len)

    mod_all = _modulation(c, ada_w, ada_b).reshape(depth, bsz, 6, d)
    cos, sin = _rope_tables(positions)
    ones = jnp.ones((bsz, s_len, QK_NOPE), F32)
    zeros = jnp.zeros((bsz, s_len, HEAD_SLAB - QK_NOPE - QK_ROPE), F32)
    cosp = jnp.concatenate([ones, cos, cos, zeros], axis=-1)
    sinp = jnp.concatenate([jnp.zeros_like(ones), sin, sin, zeros], axis=-1)
    row = lambda a: a.reshape(1, -1)

    for l in range(depth):
        mod = mod_all[l]
        w_a, w_c, wuqa, wuqb, wuk, wuv, gb = _layer_weights(w_in[l], w_uq[l], w_ukv[l],
                                                           mlstm_gate_b[l])
        q, k, v, qm, km, vm, gates = _in_proj(
            x, mod, row(norm_mix_w[l]), w_a, row(q_norm_w[l]), row(kv_norm_w[l]), wuqa, wuqb,
            wuk, wuv, mlstm_conv_w[l], row(mlstm_conv_b[l]), gb, cosp, sinp, tm=tm)
        y_mla = _attention(q, k, v, tq=tq)
        hm = _mlstm(qm, km, vm, gates, row(mlstm_head_norm_w[l]), tl=tl)
        x = _mix_out(x, mod, row(norm_mix_w[l]), w_c, y_mla, hm, w_br_mla[l].astype(BF16),
                     w_br_mlstm[l].astype(BF16), w_out[l].astype(BF16), tm=tm)
        wup, cw, cb, wdn = _ffn_weights(ffn_w_up[l], ffn_conv_w[l], ffn_conv_b[l], ffn_w_down[l])
        x = _ffn(x, mod, row(norm_ffn_w[l]), wup, cw, cb, wdn, row(final_norm_w), tm=tm,
                 final_norm=(l == depth - 1))
    return x
```

```python
import functools

import jax
import jax.numpy as jnp
from jax import lax
from jax.experimental import pallas as pl
from jax.experimental.pallas import tpu as pltpu

F32 = jnp.float32
BF16 = jnp.bfloat16

D_MODEL = 1024
CHUNK = 64
MLA_HEADS = 8
Q_LORA = 384
KV_LORA = 256
QK_NOPE = 64
QK_ROPE = 32
V_HEAD = 64
ROPE_THETA = 10000.0
MLSTM_HEADS = 8
MLSTM_HEAD_DIM = 64
MLSTM_W = MLSTM_HEADS * MLSTM_HEAD_DIM
MLSTM_CONV = 4
D_FF = 2816
FFN_CONV = 3
MLA_W = MLA_HEADS * V_HEAD
EPS = 1e-6

LANES = 128
SUBLANES = 8
HEAD_SLAB = 128
PAIR = 2 * MLSTM_HEAD_DIM
NEG = -1e30
LOG2_E = 1.4426950408889634

A_CQ = 0
A_CKV = A_CQ + Q_LORA
A_KR = A_CKV + KV_LORA
A_KRR = A_KR + LANES
A_QK = A_KRR + LANES
A_V = A_QK + 2 * MLSTM_W
A_G = A_V + MLSTM_W
A_COLS = A_G + LANES
C_O = 0
C_GA = C_O + MLSTM_W
C_GM = C_GA + D_MODEL
C_COLS = C_GM + D_MODEL

FF_CHUNK = 256
N_FF = D_FF // FF_CHUNK

VMEM_LIMIT = 56 * 1024 * 1024


def _params(n_axes):
    return pltpu.CompilerParams(dimension_semantics=("arbitrary",) * n_axes,
                                vmem_limit_bytes=VMEM_LIMIT)


def _resident(shape):
    nd = len(shape)
    return pl.BlockSpec(shape, lambda *_: (0,) * nd, pipeline_mode=pl.Buffered(1))


def _rms(x):
    return x * lax.rsqrt(jnp.mean(x * x, axis=-1, keepdims=True) + EPS)


def _dot(a, b):
    return jnp.dot(a, b, preferred_element_type=F32)


def _dot_nt(a, b):
    return lax.dot_general(a, b, (((1,), (1,)), ((), ())), preferred_element_type=F32)


def _mod_kernel(c_ref, w_ref, b_ref, o_ref):
    c = c_ref[...]
    c_act = (c * jax.nn.sigmoid(c)).astype(BF16)
    o_ref[0] = _dot(c_act, w_ref[0].astype(BF16)) + b_ref[0]


def _modulation(c, ada_w, ada_b):
    depth, d, n = ada_w.shape
    bsz = c.shape[0]
    tn = 1024
    return pl.pallas_call(
        _mod_kernel,
        grid=(depth, n // tn),
        in_specs=[pl.BlockSpec((bsz, d), lambda l, j: (0, 0)),
                  pl.BlockSpec((1, d, tn), lambda l, j: (l, 0, j)),
                  pl.BlockSpec((1, 1, tn), lambda l, j: (l, 0, j))],
        out_specs=pl.BlockSpec((1, bsz, tn), lambda l, j: (l, 0, j)),
        out_shape=jax.ShapeDtypeStruct((depth, bsz, n), F32),
        compiler_params=_params(2),
        name="modulation",
    )(c, ada_w, ada_b.reshape(depth, 1, n))


def _rope_kernel(pos_ref, inv_ref, cos_ref, sin_ref):
    ang = pos_ref[...].astype(F32) * inv_ref[...]
    cos_ref[...] = jnp.cos(ang)
    sin_ref[...] = jnp.sin(ang)


def _rope_tables(positions):
    bsz, s_len = positions.shape
    half = QK_ROPE // 2
    per_row = LANES // half
    rows = bsz * s_len // per_row
    pos = jnp.repeat(positions.reshape(rows, per_row), half, axis=1)
    inv = ROPE_THETA ** (-jnp.arange(half, dtype=F32) / half)
    inv = jnp.tile(inv, per_row).reshape(1, LANES)
    tr = min(rows, 512)
    cos, sin = pl.pallas_call(
        _rope_kernel,
        grid=(rows // tr,),
        in_specs=[pl.BlockSpec((tr, LANES), lambda i: (i, 0)),
                  pl.BlockSpec((1, LANES), lambda i: (0, 0))],
        out_specs=[pl.BlockSpec((tr, LANES), lambda i: (i, 0))] * 2,
        out_shape=[jax.ShapeDtypeStruct((rows, LANES), F32)] * 2,
        compiler_params=_params(1),
        name="rope_tables",
    )(pos, inv)
    return cos.reshape(bsz, s_len, half), sin.reshape(bsz, s_len, half)


def _in_proj_kernel(x_ref, mod_ref, nw_ref, win_ref, qnw_ref, kvnw_ref, wuqa_ref, wuqb_ref,
                    wuk_ref, wuv_ref, cw_ref, cb_ref, gb_ref, cos_ref, sin_ref,
                    q_ref, k_ref, v_ref, qm_ref, km_ref, vm_ref, g_ref, cbuf, *, tm):
    halo = SUBLANES
    mod = mod_ref[0]
    h = _rms(x_ref[0]) * nw_ref[...] * (1.0 + mod[1:2]) + mod[0:1]
    hb = h.astype(BF16)
    cos = cos_ref[0]
    sin = sin_ref[0]

    def proj(lo, width):
        return _dot(hb, win_ref[:, lo:lo + width])

    cqn = (_rms(proj(A_CQ, Q_LORA)) * qnw_ref[...]).astype(BF16)
    qa = _dot(cqn, wuqa_ref[...])
    qb = _dot(cqn, wuqb_ref[...])
    scale = (QK_NOPE + QK_ROPE) ** -0.5 * LOG2_E
    for hd in range(MLA_HEADS):
        sl = slice(hd * HEAD_SLAB, (hd + 1) * HEAD_SLAB)
        q_ref[0, :, sl] = ((qa[:, sl] * cos + qb[:, sl] * sin) * scale).astype(BF16)

    ckvn = (_rms(proj(A_CKV, KV_LORA)) * kvnw_ref[...]).astype(BF16)
    kn = _dot(ckvn, wuk_ref[...])
    kr = proj(A_KR, LANES) * cos + proj(A_KRR, LANES) * sin
    for hd in range(MLA_HEADS):
        sl = slice(hd * HEAD_SLAB, (hd + 1) * HEAD_SLAB)
        k_ref[0, :, sl] = (kn[:, sl] + kr).astype(BF16)
    slab_row = lax.broadcasted_iota(jnp.int32, (MLA_HEADS * HEAD_SLAB, 1), 0) % HEAD_SLAB
    v_ref[0, 0] = jnp.where(slab_row == V_HEAD, 1.0, _dot_nt(wuv_ref[...], ckvn)).astype(BF16)

    @pl.when(pl.program_id(1) == 0)
    def _():
        cbuf[0:halo, :] = jnp.zeros((halo, 2 * MLSTM_W), F32)

    cbuf[halo:halo + tm, :] = proj(A_QK, 2 * MLSTM_W)
    y = cb_ref[...] + cw_ref[3:4, :] * cbuf[halo:halo + tm, :]
    for j in range(1, MLSTM_CONV):
        y = y + cw_ref[3 - j:4 - j, :] * cbuf[halo - j:halo - j + tm, :]
    cbuf[0:halo, :] = cbuf[tm:tm + halo, :]
    y = y * jax.nn.sigmoid(y)
    qm_ref[0] = y[:, :MLSTM_W].astype(BF16)
    km_ref[0] = (y[:, MLSTM_W:] * (MLSTM_HEAD_DIM ** -0.5)).astype(BF16)
    vm_ref[0] = proj(A_V, MLSTM_W).astype(BF16)

    g = proj(A_G, LANES) + gb_ref[...]
    logf = jnp.minimum(g, 0.0) - jnp.log1p(jnp.exp(-jnp.abs(g)))
    lane = lax.broadcasted_iota(jnp.int32, g.shape, 1)
    g_ref[0] = jnp.where(lane < MLSTM_HEADS, g, logf)


def _in_proj(x, mod, nw, w_a, qnw, kvnw, wuqa, wuqb, wuk, wuv, cw, cb, gb, cosp, sinp, *, tm):
    bsz, s_len, d = x.shape
    row = lambda w: pl.BlockSpec((1, tm, w), lambda b, s: (b, s, 0))
    slabs = MLA_HEADS * HEAD_SLAB
    out_widths = (slabs, slabs, LANES, MLSTM_W, MLSTM_W, MLSTM_W, LANES)
    out_shape = [jax.ShapeDtypeStruct((bsz, s_len, w), BF16) for w in out_widths]
    out_specs = [row(w) for w in out_widths]
    out_shape[2] = jax.ShapeDtypeStruct((bsz, s_len // tm, slabs, tm), BF16)
    out_specs[2] = pl.BlockSpec((1, 1, slabs, tm), lambda b, s: (b, s, 0, 0))
    out_shape[6] = jax.ShapeDtypeStruct((bsz, s_len, LANES), F32)
    out_specs[6] = row(LANES)
    return pl.pallas_call(
        functools.partial(_in_proj_kernel, tm=tm),
        grid=(bsz, s_len // tm),
        in_specs=[row(d),
                  pl.BlockSpec((1, 6, d), lambda b, s: (b, 0, 0)),
                  _resident(nw.shape), _resident(w_a.shape), _resident(qnw.shape),
                  _resident(kvnw.shape), _resident(wuqa.shape), _resident(wuqb.shape),
                  _resident(wuk.shape), _resident(wuv.shape), _resident(cw.shape),
                  _resident(cb.shape), _resident(gb.shape), row(LANES), row(LANES)],
        out_specs=out_specs,
        out_shape=out_shape,
        scratch_shapes=[pltpu.VMEM((tm + 2 * SUBLANES, 2 * MLSTM_W), F32)],
        compiler_params=_params(2),
        name="in_proj",
    )(x, mod, nw, w_a, qnw, kvnw, wuqa, wuqb, wuk, wuv, cw, cb, gb, cosp, sinp)


def _attn_kernel(q_ref, k_ref, vt_ref, o_ref, m_ref, acc_ref, *, tq):
    qi = pl.program_id(2)
    key_chunk = lax.broadcasted_iota(jnp.int32, (tq, 1), 0) // CHUNK
    qry_chunk = lax.broadcasted_iota(jnp.int32, (1, tq), 1) // CHUNK
    visible = key_chunk <= qry_chunk

    def block(j, diagonal):
        start = pl.multiple_of(j * tq, tq)
        for hh in range(2):
            sl = slice(hh * HEAD_SLAB, (hh + 1) * HEAD_SLAB)
            st = _dot_nt(k_ref[0, pl.ds(start, tq), sl], q_ref[0, :, sl])
            vt = vt_ref[0, j, sl, :]
            if diagonal:
                st = jnp.where(visible, st, NEG)
                m_new = jnp.max(st, axis=0, keepdims=True)
                acc_ref[hh] = _dot(vt, jnp.exp2(st - m_new).astype(BF16))
            else:
                m_old = m_ref[hh]
                m_new = jnp.maximum(m_old, jnp.max(st, axis=0, keepdims=True))
                pv = _dot(vt, jnp.exp2(st - m_new).astype(BF16))
                acc_ref[hh] = jnp.exp2(m_old - m_new) * acc_ref[hh] + pv
            m_ref[hh] = m_new

    block(qi, True)

    def body(j, carry):
        block(j, False)
        return carry

    lax.fori_loop(0, qi, body, 0)
    outs = [acc_ref[hh, 0:V_HEAD, :] / acc_ref[hh, V_HEAD:V_HEAD + 1, :] for hh in range(2)]
    o_ref[0] = jnp.concatenate(outs, axis=0).T.astype(BF16)


def _attention(q, k, vt, *, tq):
    bsz, s_len, _ = q.shape
    pair = 2 * HEAD_SLAB
    n_kv = s_len // tq
    return pl.pallas_call(
        functools.partial(_attn_kernel, tq=tq),
        grid=(bsz, MLA_HEADS // 2, n_kv),
        in_specs=[pl.BlockSpec((1, tq, pair), lambda b, h, i: (b, i, h)),
                  pl.BlockSpec((1, s_len, pair), lambda b, h, i: (b, 0, h)),
                  pl.BlockSpec((1, n_kv, pair, tq), lambda b, h, i: (b, 0, h, 0))],
        out_specs=pl.BlockSpec((1, tq, 2 * V_HEAD), lambda b, h, i: (b, i, h)),
        out_shape=jax.ShapeDtypeStruct((bsz, s_len, MLA_W), BF16),
        scratch_shapes=[pltpu.VMEM((2, 1, tq), F32), pltpu.VMEM((2, HEAD_SLAB, tq), F32)],
        compiler_params=_params(3),
        name="mla_attention",
    )(q, k, vt)


def _split3(a):
    hi = a.astype(BF16)
    r = a - hi.astype(F32)
    mid = r.astype(BF16)
    lo = (r - mid.astype(F32)).astype(BF16)
    return hi, mid, lo


def _mlstm_kernel(q_ref, k_ref, v_ref, g_ref, nw_ref, o_ref, c_st, n_st, m_st, *, tl):
    @pl.when(pl.program_id(1) == 0)
    def _():
        c_st[...] = jnp.zeros(c_st.shape, F32)
        n_st[...] = jnp.zeros(n_st.shape, F32)
        m_st[...] = jnp.zeros(m_st.shape, F32)

    nh = MLSTM_HEADS
    g = g_ref[0]
    row = lax.broadcasted_iota(jnp.int32, (tl, tl), 0)
    col = lax.broadcasted_iota(jnp.int32, (tl, tl), 1)
    causal = col <= row
    tri = jnp.where(causal, 1.0, 0.0).astype(BF16)
    b_all = sum(_dot(tri, part) for part in _split3(g))
    g_t = g.T
    b_t = b_all.T
    lane = lax.broadcasted_iota(jnp.int32, (tl, PAIR), 1)
    first = lane < MLSTM_HEAD_DIM
    lane_row = lax.broadcasted_iota(jnp.int32, (1, PAIR), 1) < MLSTM_HEAD_DIM
    blk = (lax.broadcasted_iota(jnp.int32, (PAIR, PAIR), 0) < MLSTM_HEAD_DIM) == \
          (lax.broadcasted_iota(jnp.int32, (PAIR, PAIR), 1) < MLSTM_HEAD_DIM)

    def pick(c0, c1):
        return jnp.where(first, c0, c1)

    for p in range(nh // 2):
        sl = slice(p * PAIR, (p + 1) * PAIR)
        qp = q_ref[0, :, sl]
        kp = k_ref[0, :, sl]
        vp = v_ref[0, :, sl]
        qf = qp.astype(F32)
        c_old = c_st[p]
        n_old = n_st[p]
        inter = _dot(qp, c_old.astype(BF16))
        nums, dens, w_inters, e_negs, w_ks, decays, m_news = [], [], [], [], [], [], []
        for hh in range(2):
            hd = 2 * p + hh
            b_col = b_all[:, nh + hd:nh + hd + 1]
            a_row = g_t[hd:hd + 1, :] - b_t[nh + hd:nh + hd + 1, :]
            a_col = g[:, hd:hd + 1] - b_col
            m_prev = m_st[p, :, hh * MLSTM_HEAD_DIM:hh * MLSTM_HEAD_DIM + 1]
            log_d = jnp.where(causal, b_col + a_row, NEG)
            log_inter = b_col + m_prev
            m_t = jnp.maximum(log_inter, jnp.max(log_d, axis=-1, keepdims=True))
            w_intra = jnp.exp(log_d - m_t)
            w_inter = jnp.exp(log_inter - m_t)
            q_h = jnp.where(first == (hh == 0), qp, jnp.zeros_like(qp))
            s = _dot_nt(q_h, kp) * w_intra
            nums.append(_dot(s.astype(BF16), vp))
            qn = jnp.sum(jnp.where(first == (hh == 0), qf * n_old, 0.0), axis=-1, keepdims=True)
            dens.append(jnp.sum(s, axis=-1, keepdims=True) + w_inter * qn)
            w_inters.append(w_inter)
            e_negs.append(jnp.exp(-m_t))
            b_last = b_col[tl - 1:tl, :]
            log_w = b_last + a_col
            m_new = jnp.maximum(b_last + m_prev, jnp.max(log_w, axis=0, keepdims=True))
            w_ks.append(jnp.exp(log_w - m_new))
            decays.append(jnp.exp(b_last + m_prev - m_new))
            m_news.append(m_new)
        num = pick(nums[0], nums[1]) + pick(w_inters[0], w_inters[1]) * inter
        den = jnp.maximum(jnp.abs(pick(dens[0], dens[1])), pick(e_negs[0], e_negs[1]))
        hp = num / den
        sq = hp * hp
        ms0 = jnp.sum(jnp.where(first, sq, 0.0), axis=-1, keepdims=True)
        ms1 = jnp.sum(jnp.where(first, 0.0, sq), axis=-1, keepdims=True)
        ms = pick(ms0, ms1) * (1.0 / MLSTM_HEAD_DIM)
        o_ref[0, :, sl] = hp * lax.rsqrt(ms + EPS) * nw_ref[:, sl]
        kw = kp.astype(F32) * pick(w_ks[0], w_ks[1])
        dec = jnp.where(lane_row, decays[0], decays[1])
        upd = _dot(kw.T.astype(BF16), vp)
        c_st[p] = dec * c_old + jnp.where(blk, upd, 0.0)
        n_st[p] = dec * n_old + jnp.sum(kw, axis=0, keepdims=True)
        m_st[p] = jnp.where(lane_row, m_news[0], m_news[1])


def _mlstm(qm, km, vm, gates, head_nw, *, tl):
    bsz, s_len, w = qm.shape
    row = lambda width: pl.BlockSpec((1, tl, width), lambda b, s: (b, s, 0))
    n_pairs = MLSTM_HEADS // 2
    return pl.pallas_call(
        functools.partial(_mlstm_kernel, tl=tl),
        grid=(bsz, s_len // tl),
        in_specs=[row(w), row(w), row(w), row(LANES), _resident(head_nw.shape)],
        out_specs=row(w),
        out_shape=jax.ShapeDtypeStruct((bsz, s_len, w), F32),
        scratch_shapes=[pltpu.VMEM((n_pairs, PAIR, PAIR), F32),
                        pltpu.VMEM((n_pairs, 1, PAIR), F32),
                        pltpu.VMEM((n_pairs, 1, PAIR), F32)],
        compiler_params=_params(2),
        name="mlstm",
    )(qm, km, vm, gates, head_nw)


def _mix_out_kernel(x_ref, mod_ref, nw_ref, wc_ref, ya_ref, hm_ref, wba_ref, wbm_ref, wo_ref,
                    o_ref):
    mod = mod_ref[0]
    x = x_ref[0]
    hb = (_rms(x) * nw_ref[...] * (1.0 + mod[1:2]) + mod[0:1]).astype(BF16)
    o_gate = jax.nn.sigmoid(_dot(hb, wc_ref[:, C_O:C_O + MLSTM_W]))
    y_mlstm = (o_gate * hm_ref[0]).astype(BF16)
    merged = jax.nn.sigmoid(_dot(hb, wc_ref[:, C_GA:C_GA + D_MODEL])) * _dot(ya_ref[0], wba_ref[...])
    merged = merged + (jax.nn.sigmoid(_dot(hb, wc_ref[:, C_GM:C_GM + D_MODEL]))
                       * _dot(y_mlstm, wbm_ref[...]))
    o_ref[0] = x + mod[2:3] * _dot(merged.astype(BF16), wo_ref[...])


def _mix_out(x, mod, nw, w_c, y_mla, hm, wba, wbm, wo, *, tm):
    bsz, s_len, d = x.shape
    row = lambda w: pl.BlockSpec((1, tm, w), lambda b, s: (b, s, 0))
    return pl.pallas_call(
        _mix_out_kernel,
        grid=(bsz, s_len // tm),
        in_specs=[row(d), pl.BlockSpec((1, 6, d), lambda b, s: (b, 0, 0)),
                  _resident(nw.shape), _resident(w_c.shape), row(MLA_W), row(MLSTM_W),
                  _resident(wba.shape), _resident(wbm.shape), _resident(wo.shape)],
        out_specs=row(d),
        out_shape=jax.ShapeDtypeStruct((bsz, s_len, d), F32),
        compiler_params=_params(2),
        name="mix_out",
    )(x, mod, nw, w_c, y_mla, hm, wba, wbm, wo)


def _gelu_tanh(a):
    return 0.5 * a * (1.0 + jnp.tanh(0.7978845608028654 * (a + 0.044715 * (a * a * a))))


def _ffn_kernel(x_ref, mod_ref, nw_ref, wup_ref, cw_ref, cb_ref, wdn_ref, fnw_ref, o_ref,
                ubuf, halo_ref, acc_ref, *, tm, final_norm):
    halo = SUBLANES
    mod = mod_ref[0]
    x = x_ref[0]
    hb = (_rms(x) * nw_ref[...] * (1.0 + mod[4:5]) + mod[3:4]).astype(BF16)

    @pl.when(pl.program_id(1) == 0)
    def _():
        halo_ref[...] = jnp.zeros(halo_ref.shape, F32)

    for c in range(N_FF):
        ubuf[0:halo, :] = halo_ref[c]
        ubuf[halo:halo + tm, :] = _dot(hb, wup_ref[c])
        halo_ref[c] = ubuf[tm:tm + halo, :]
        u = cb_ref[c] + cw_ref[c, 2:3, :] * ubuf[halo:halo + tm, :]
        for j in range(1, FFN_CONV):
            u = u + cw_ref[c, 2 - j:3 - j, :] * ubuf[halo - j:halo - j + tm, :]
        gated = (_gelu_tanh(u[:, :FF_CHUNK]) * u[:, FF_CHUNK:]).astype(BF16)
        part = _dot(gated, wdn_ref[c])
        if c == 0:
            acc_ref[...] = part
        else:
            acc_ref[...] += part
    y = x + mod[5:6] * acc_ref[...]
    if final_norm:
        y = _rms(y) * fnw_ref[...]
    o_ref[0] = y


def _ffn(x, mod, nw, wup, cw, cb, wdn, fnw, *, tm, final_norm):
    bsz, s_len, d = x.shape
    row = lambda w: pl.BlockSpec((1, tm, w), lambda b, s: (b, s, 0))
    return pl.pallas_call(
        functools.partial(_ffn_kernel, tm=tm, final_norm=final_norm),
        grid=(bsz, s_len // tm),
        in_specs=[row(d), pl.BlockSpec((1, 6, d), lambda b, s: (b, 0, 0)),
                  _resident(nw.shape), _resident(wup.shape), _resident(cw.shape),
                  _resident(cb.shape), _resident(wdn.shape), _resident(fnw.shape)],
        out_specs=row(d),
        out_shape=jax.ShapeDtypeStruct((bsz, s_len, d), F32),
        scratch_shapes=[pltpu.VMEM((tm + 2 * SUBLANES, 2 * FF_CHUNK), F32),
                        pltpu.VMEM((N_FF, SUBLANES, 2 * FF_CHUNK), F32),
                        pltpu.VMEM((tm, d), F32)],
        compiler_params=_params(2),
        name="conv_ffn",
    )(x, mod, nw, wup, cw, cb, wdn, fnw)


def _rot_cols(w):
    half = w.shape[-1] // 2
    return jnp.concatenate([-w[..., half:], w[..., :half]], axis=-1)


def _pad_cols(w, before, total):
    pad = [(0, 0)] * (w.ndim - 1) + [(before, total - before - w.shape[-1])]
    return jnp.pad(w, pad)


def _layer_weights(w_in, w_uq, w_ukv, gate_b):
    d = w_in.shape[0]
    o = 0
    seg = {}
    for name, width in (("cq", Q_LORA), ("ckv", KV_LORA), ("kr", QK_ROPE), ("qm", MLSTM_W),
                        ("km", MLSTM_W), ("vm", MLSTM_W), ("om", MLSTM_W), ("im", MLSTM_HEADS),
                        ("fm", MLSTM_HEADS), ("ga", D_MODEL), ("gm", D_MODEL)):
        seg[name] = w_in[:, o:o + width]
        o += width
    gates = _pad_cols(jnp.concatenate([seg["im"], seg["fm"]], axis=1), 0, LANES)
    w_a = jnp.concatenate([seg["cq"], seg["ckv"],
                           _pad_cols(seg["kr"], QK_NOPE, LANES),
                           _pad_cols(_rot_cols(seg["kr"]), QK_NOPE, LANES),
                           seg["qm"], seg["km"], seg["vm"], gates], axis=1).astype(BF16)
    w_c = jnp.concatenate([seg["om"], seg["ga"], seg["gm"]], axis=1).astype(BF16)
    uq = w_uq.reshape(Q_LORA, MLA_HEADS, QK_NOPE + QK_ROPE)
    wuqa = _pad_cols(uq, 0, HEAD_SLAB).reshape(Q_LORA, -1).astype(BF16)
    wuqb = _pad_cols(_rot_cols(uq[..., QK_NOPE:]), QK_NOPE, HEAD_SLAB).reshape(Q_LORA, -1).astype(BF16)
    ukv = w_ukv.reshape(KV_LORA, MLA_HEADS, QK_NOPE + V_HEAD)
    wuk = _pad_cols(ukv[..., :QK_NOPE], 0, HEAD_SLAB).reshape(KV_LORA, -1).astype(BF16)
    wuv = _pad_cols(ukv[..., QK_NOPE:], 0, HEAD_SLAB).reshape(KV_LORA, -1).T.astype(BF16)
    gb = _pad_cols(gate_b.reshape(1, -1), 0, LANES)
    assert w_a.shape == (d, A_COLS) and w_c.shape == (d, C_COLS)
    return w_a, w_c, wuqa, wuqb, wuk, wuv, gb


def _ffn_weights(w_up, conv_w, conv_b, w_down):
    def chunked(a):
        lead = a.shape[:-1]
        a = a.reshape(lead + (2, N_FF, FF_CHUNK))
        a = jnp.moveaxis(a, -2, 0)
        return a.reshape((N_FF,) + lead + (2 * FF_CHUNK,))
    return (chunked(w_up).astype(BF16), chunked(conv_w), chunked(conv_b.reshape(1, -1)),
            w_down.reshape(N_FF, FF_CHUNK, -1).astype(BF16))


def kernel(x, c, positions, ada_w, ada_b, norm_mix_w, w_in, q_norm_w, kv_norm_w, w_uq, w_ukv,
           mlstm_conv_w, mlstm_conv_b, mlstm_gate_b, mlstm_head_norm_w, w_br_mla, w_br_mlstm,
           w_out, norm_ffn_w, ffn_w_up, ffn_conv_w, ffn_conv_b, ffn_w_down, final_norm_w):
    bsz, s_len, d = x.shape
    depth = ada_w.shape[0]
    tm = min(512, s_len)
    tq = min(512, s_len)
    tl = min(256, s_len)

    mod_all = _modulation(c, ada_w, ada_b).reshape(depth, bsz, 6, d)
    cos, sin = _rope_tables(positions)
    ones = jnp.ones((bsz, s_len, QK_NOPE), F32)
    zeros = jnp.zeros((bsz, s_len, HEAD_SLAB - QK_NOPE - QK_ROPE), F32)
    cosp = jnp.concatenate([ones, cos, cos, zeros], axis=-1)
    sinp = jnp.concatenate([jnp.zeros_like(ones), sin, sin, zeros], axis=-1)
    row = lambda a: a.reshape(1, -1)

    for l in range(depth):
        mod = mod_all[l]
        w_a, w_c, wuqa, wuqb, wuk, wuv, gb = _layer_weights(w_in[l], w_uq[l], w_ukv[l],
                                                           mlstm_gate_b[l])
        q, k, v, qm, km, vm, gates = _in_proj(
            x, mod, row(norm_mix_w[l]), w_a, row(q_norm_w[l]), row(kv_norm_w[l]), wuqa, wuqb,
            wuk, wuv, mlstm_conv_w[l], row(mlstm_conv_b[l]), gb, cosp, sinp, tm=tm)
        y_mla = _attention(q, k, v, tq=tq)
        hm = _mlstm(qm, km, vm, gates, row(mlstm_head_norm_w[l]), tl=tl)
        x = _mix_out(x, mod, row(norm_mix_w[l]), w_c, y_mla, hm, w_br_mla[l].astype(BF16),
                     w_br_mlstm[l].astype(BF16), w_out[l].astype(BF16), tm=tm)
        wup, cw, cb, wdn = _ffn_weights(ffn_w_up[l], ffn_conv_w[l], ffn_conv_b[l], ffn_w_down[l])
        x = _ffn(x, mod, row(norm_ffn_w[l]), wup, cw, cb, wdn, row(final_norm_w), tm=tm,
                 final_norm=(l == depth - 1))
    return x
```

```python
import functools

import jax
import jax.numpy as jnp
from jax import lax
from jax.experimental import pallas as pl
from jax.experimental.pallas import tpu as pltpu

F32 = jnp.float32
BF16 = jnp.bfloat16

D_MODEL = 1024
CHUNK = 64
MLA_HEADS = 8
Q_LORA = 384
KV_LORA = 256
QK_NOPE = 64
QK_ROPE = 32
V_HEAD = 64
ROPE_THETA = 10000.0
MLSTM_HEADS = 8
MLSTM_HEAD_DIM = 64
MLSTM_W = MLSTM_HEADS * MLSTM_HEAD_DIM
MLSTM_CONV = 4
D_FF = 2816
FFN_CONV = 3
MLA_W = MLA_HEADS * V_HEAD
EPS = 1e-6

LANES = 128
SUBLANES = 8
HEAD_SLAB = 128
PAIR = 2 * MLSTM_HEAD_DIM
NEG = -1e30
LOG2_E = 1.4426950408889634

A_CQ = 0
A_CKV = A_CQ + Q_LORA
A_KR = A_CKV + KV_LORA
A_KRR = A_KR + LANES
A_QK = A_KRR + LANES
A_G = A_QK + 2 * MLSTM_W
A_COLS = A_G + LANES
C_O = 0
C_GA = C_O + MLSTM_W
C_GM = C_GA + D_MODEL
C_COLS = C_GM + D_MODEL

FF_CHUNK = 256
N_FF = D_FF // FF_CHUNK

VMEM_LIMIT = 56 * 1024 * 1024


def _params(n_axes):
    return pltpu.CompilerParams(dimension_semantics=("arbitrary",) * n_axes,
                                vmem_limit_bytes=VMEM_LIMIT)


def _resident(shape):
    nd = len(shape)
    return pl.BlockSpec(shape, lambda *_: (0,) * nd, pipeline_mode=pl.Buffered(1))


def _rms(x):
    return x * lax.rsqrt(jnp.mean(x * x, axis=-1, keepdims=True) + EPS)


def _dot(a, b):
    return jnp.dot(a, b, preferred_element_type=F32)


def _dot_nt(a, b):
    return lax.dot_general(a, b, (((1,), (1,)), ((), ())), preferred_element_type=F32)


def _causal_conv(u, prev, taps, bias):
    width = len(taps)
    first = lax.broadcasted_iota(jnp.int32, (SUBLANES, 1), 0)
    y = bias + taps[width - 1] * u
    for j in range(1, width):
        shifted = pltpu.roll(u, j, axis=0)
        head = jnp.where(first < j, pltpu.roll(prev, j, axis=0), shifted[0:SUBLANES])
        shifted = jnp.concatenate([head, shifted[SUBLANES:]], axis=0)
        y = y + taps[width - 1 - j] * shifted
    return y


def _mod_kernel(c_ref, w_ref, b_ref, o_ref):
    c = c_ref[...]
    c_act = (c * jax.nn.sigmoid(c)).astype(BF16)
    o_ref[0] = _dot(c_act, w_ref[0].astype(BF16)) + b_ref[0]


def _modulation(c, ada_w, ada_b):
    depth, d, n = ada_w.shape
    bsz = c.shape[0]
    tn = 1024
    return pl.pallas_call(
        _mod_kernel,
        grid=(depth, n // tn),
        in_specs=[pl.BlockSpec((bsz, d), lambda l, j: (0, 0)),
                  pl.BlockSpec((1, d, tn), lambda l, j: (l, 0, j)),
                  pl.BlockSpec((1, 1, tn), lambda l, j: (l, 0, j))],
        out_specs=pl.BlockSpec((1, bsz, tn), lambda l, j: (l, 0, j)),
        out_shape=jax.ShapeDtypeStruct((depth, bsz, n), F32),
        compiler_params=_params(2),
        name="modulation",
    )(c, ada_w, ada_b.reshape(depth, 1, n))


def _rope_kernel(pos_ref, inv_ref, cos_ref, sin_ref):
    ang = pos_ref[...].astype(F32) * inv_ref[...]
    cos_ref[...] = jnp.cos(ang)
    sin_ref[...] = jnp.sin(ang)


def _rope_tables(positions):
    bsz, s_len = positions.shape
    half = QK_ROPE // 2
    per_row = LANES // half
    rows = bsz * s_len // per_row
    pos = jnp.repeat(positions.reshape(rows, per_row), half, axis=1)
    inv = ROPE_THETA ** (-jnp.arange(half, dtype=F32) / half)
    inv = jnp.tile(inv, per_row).reshape(1, LANES)
    tr = min(rows, 512)
    cos, sin = pl.pallas_call(
        _rope_kernel,
        grid=(rows // tr,),
        in_specs=[pl.BlockSpec((tr, LANES), lambda i: (i, 0)),
                  pl.BlockSpec((1, LANES), lambda i: (0, 0))],
        out_specs=[pl.BlockSpec((tr, LANES), lambda i: (i, 0))] * 2,
        out_shape=[jax.ShapeDtypeStruct((rows, LANES), F32)] * 2,
        compiler_params=_params(1),
        name="rope_tables",
    )(pos, inv)
    return cos.reshape(bsz, s_len, half), sin.reshape(bsz, s_len, half)


def _in_proj_kernel(x_ref, mod_ref, nw_ref, win_ref, qnw_ref, kvnw_ref, wuqa_ref, wuqb_ref,
                    wuk_ref, wuv_ref, wvm_ref, cw_ref, cb_ref, gb_ref, cos_ref, sin_ref,
                    q_ref, k_ref, v_ref, qm_ref, km_ref, vm_ref, g_ref, halo_ref, *, tm, tl):
    mod = mod_ref[0]
    h = _rms(x_ref[0]) * nw_ref[...] * (1.0 + mod[1:2]) + mod[0:1]
    hb = h.astype(BF16)
    cos = cos_ref[0]
    sin = sin_ref[0]

    def proj(lo, width):
        return _dot(hb, win_ref[:, lo:lo + width])

    cqn = (_rms(proj(A_CQ, Q_LORA)) * qnw_ref[...]).astype(BF16)
    qa = _dot(cqn, wuqa_ref[...])
    qb = _dot(cqn, wuqb_ref[...])
    scale = (QK_NOPE + QK_ROPE) ** -0.5 * LOG2_E
    for hd in range(MLA_HEADS):
        sl = slice(hd * HEAD_SLAB, (hd + 1) * HEAD_SLAB)
        q_ref[0, :, sl] = ((qa[:, sl] * cos + qb[:, sl] * sin) * scale).astype(BF16)

    ckvn = (_rms(proj(A_CKV, KV_LORA)) * kvnw_ref[...]).astype(BF16)
    kn = _dot(ckvn, wuk_ref[...])
    kr = proj(A_KR, LANES) * cos + proj(A_KRR, LANES) * sin
    for hd in range(MLA_HEADS):
        sl = slice(hd * HEAD_SLAB, (hd + 1) * HEAD_SLAB)
        k_ref[0, :, sl] = (kn[:, sl] + kr).astype(BF16)
    slab_row = lax.broadcasted_iota(jnp.int32, (MLA_HEADS * HEAD_SLAB, 1), 0) % HEAD_SLAB
    v_ref[0, 0] = jnp.where(slab_row == V_HEAD, 1.0, _dot_nt(wuv_ref[...], ckvn)).astype(BF16)

    @pl.when(pl.program_id(1) == 0)
    def _():
        halo_ref[...] = jnp.zeros(halo_ref.shape, F32)

    qk = proj(A_QK, 2 * MLSTM_W)
    prev = halo_ref[...]
    halo_ref[...] = qk[tm - SUBLANES:tm, :]
    y = _causal_conv(qk, prev, [cw_ref[j:j + 1, :] for j in range(MLSTM_CONV)], cb_ref[...])
    y = y * jax.nn.sigmoid(y)
    qm_ref[0] = y[:, :MLSTM_W].astype(BF16)
    km_ref[0] = (y[:, MLSTM_W:] * (MLSTM_HEAD_DIM ** -0.5)).astype(BF16)
    vmt = _dot_nt(wvm_ref[...], hb).astype(BF16)
    for c in range(tm // tl):
        vm_ref[0, c] = vmt[:, c * tl:(c + 1) * tl]

    g = proj(A_G, LANES) + gb_ref[...]
    logf = jnp.minimum(g, 0.0) - jnp.log1p(jnp.exp(-jnp.abs(g)))
    lane = lax.broadcasted_iota(jnp.int32, g.shape, 1)
    g_ref[0] = jnp.where(lane < MLSTM_HEADS, g, logf)


def _in_proj(x, mod, nw, w_a, qnw, kvnw, wuqa, wuqb, wuk, wuv, wvm, cw, cb, gb, cosp, sinp, *, tm, tl):
    bsz, s_len, d = x.shape
    row = lambda w: pl.BlockSpec((1, tm, w), lambda b, s: (b, s, 0))
    slabs = MLA_HEADS * HEAD_SLAB
    out_widths = (slabs, slabs, LANES, MLSTM_W, MLSTM_W, LANES, LANES)
    out_shape = [jax.ShapeDtypeStruct((bsz, s_len, w), BF16) for w in out_widths]
    out_specs = [row(w) for w in out_widths]
    out_shape[2] = jax.ShapeDtypeStruct((bsz, s_len // tm, slabs, tm), BF16)
    out_specs[2] = pl.BlockSpec((1, 1, slabs, tm), lambda b, s: (b, s, 0, 0))
    out_shape[5] = jax.ShapeDtypeStruct((bsz, s_len // tl, MLSTM_W, tl), BF16)
    out_specs[5] = pl.BlockSpec((1, tm // tl, MLSTM_W, tl), lambda b, s: (b, s, 0, 0))
    out_shape[6] = jax.ShapeDtypeStruct((bsz, s_len, LANES), F32)
    out_specs[6] = row(LANES)
    return pl.pallas_call(
        functools.partial(_in_proj_kernel, tm=tm, tl=tl),
        grid=(bsz, s_len // tm),
        in_specs=[row(d),
                  pl.BlockSpec((1, 6, d), lambda b, s: (b, 0, 0)),
                  _resident(nw.shape), _resident(w_a.shape), _resident(qnw.shape),
                  _resident(kvnw.shape), _resident(wuqa.shape), _resident(wuqb.shape),
                  _resident(wuk.shape), _resident(wuv.shape), _resident(wvm.shape),
                  _resident(cw.shape),
                  _resident(cb.shape), _resident(gb.shape), row(LANES), row(LANES)],
        out_specs=out_specs,
        out_shape=out_shape,
        scratch_shapes=[pltpu.VMEM((SUBLANES, 2 * MLSTM_W), F32)],
        compiler_params=_params(2),
        name="in_proj",
    )(x, mod, nw, w_a, qnw, kvnw, wuqa, wuqb, wuk, wuv, wvm, cw, cb, gb, cosp, sinp)


def _attn_kernel(q_ref, k_ref, vt_ref, o_ref, s_buf, mx_buf, m_ref, acc_ref, *, tq):
    qi = pl.program_id(2)
    key_chunk = lax.broadcasted_iota(jnp.int32, (tq, 1), 0) // CHUNK
    qry_chunk = lax.broadcasted_iota(jnp.int32, (1, tq), 1) // CHUNK
    visible = key_chunk <= qry_chunk
    heads = [slice(hh * HEAD_SLAB, (hh + 1) * HEAD_SLAB) for hh in range(2)]

    def scores(j, hh, diagonal):
        start = pl.multiple_of(j * tq, tq)
        st = _dot_nt(k_ref[0, pl.ds(start, tq), heads[hh]], q_ref[0, :, heads[hh]])
        if diagonal:
            st = jnp.where(visible, st, NEG)
        s_buf[hh] = st
        mx_buf[hh] = jnp.max(st, axis=0, keepdims=True)

    def accumulate(j, hh):
        m_old = m_ref[hh]
        m_new = jnp.maximum(m_old, mx_buf[hh])
        p = jnp.exp2(s_buf[hh] - m_new).astype(BF16)
        acc_ref[hh] = jnp.exp2(m_old - m_new) * acc_ref[hh] + _dot(vt_ref[0, j, heads[hh], :], p)
        m_ref[hh] = m_new

    def full_block(j, next_diagonal):
        scores(j, 1, False)
        accumulate(j, 0)
        scores(j + 1, 0, next_diagonal)
        accumulate(j, 1)

    m_ref[...] = jnp.full(m_ref.shape, NEG, F32)
    acc_ref[...] = jnp.zeros(acc_ref.shape, F32)

    @pl.when(qi == 0)
    def _():
        scores(0, 0, True)

    @pl.when(qi > 0)
    def _():
        scores(0, 0, False)

        def body(j, carry):
            full_block(j, False)
            return carry

        lax.fori_loop(0, qi - 1, body, 0)
        full_block(qi - 1, True)

    scores(qi, 1, True)
    accumulate(qi, 0)
    accumulate(qi, 1)
    outs = [acc_ref[hh, 0:V_HEAD, :] / acc_ref[hh, V_HEAD:V_HEAD + 1, :] for hh in range(2)]
    o_ref[0] = jnp.concatenate(outs, axis=0).T.astype(BF16)


def _attention(q, k, vt, *, tq):
    bsz, s_len, _ = q.shape
    pair = 2 * HEAD_SLAB
    n_kv = s_len // tq
    return pl.pallas_call(
        functools.partial(_attn_kernel, tq=tq),
        grid=(bsz, MLA_HEADS // 2, n_kv),
        in_specs=[pl.BlockSpec((1, tq, pair), lambda b, h, i: (b, i, h)),
                  pl.BlockSpec((1, s_len, pair), lambda b, h, i: (b, 0, h)),
                  pl.BlockSpec((1, n_kv, pair, tq), lambda b, h, i: (b, 0, h, 0))],
        out_specs=pl.BlockSpec((1, tq, 2 * V_HEAD), lambda b, h, i: (b, i, h)),
        out_shape=jax.ShapeDtypeStruct((bsz, s_len, MLA_W), BF16),
        scratch_shapes=[pltpu.VMEM((2, tq, tq), F32), pltpu.VMEM((2, 1, tq), F32),
                        pltpu.VMEM((2, 1, tq), F32), pltpu.VMEM((2, HEAD_SLAB, tq), F32)],
        compiler_params=_params(3),
        name="mla_attention",
    )(q, k, vt)


def _split3(a):
    hi = a.astype(BF16)
    r = a - hi.astype(F32)
    mid = r.astype(BF16)
    lo = (r - mid.astype(F32)).astype(BF16)
    return hi, mid, lo


def _mlstm_kernel(q_ref, k_ref, vt_ref, g_ref, nw_ref, o_ref, c_st, n_st, m_st, *, tl):
    @pl.when(pl.program_id(1) == 0)
    def _():
        c_st[...] = jnp.zeros(c_st.shape, F32)
        n_st[...] = jnp.zeros(n_st.shape, F32)
        m_st[...] = jnp.zeros(m_st.shape, F32)

    nh, hd_w = MLSTM_HEADS, MLSTM_HEAD_DIM
    g = g_ref[0] * LOG2_E
    key = lax.broadcasted_iota(jnp.int32, (tl, 1), 0)
    qry = lax.broadcasted_iota(jnp.int32, (1, tl), 1)
    visible = key <= qry
    tri = jnp.where(qry <= key, 1.0, 0.0).astype(BF16)
    b_all = sum(_dot(tri, part) for part in _split3(g))
    a_cols = g - pltpu.roll(b_all, LANES - nh, axis=1)
    i_rows = g.T[0:nh]
    b_rows = b_all.T[nh:2 * nh]
    a_rows = i_rows - b_rows
    m_prev = m_st[:, 0:1]
    b_last = b_rows[:, tl - 1:tl]
    log_inter = b_rows + m_prev
    log_w = b_last + a_rows
    m_new = jnp.maximum(b_last + m_prev, jnp.max(log_w, axis=1, keepdims=True))
    w_keys = jnp.exp2(log_w - m_new)
    decay = jnp.exp2(b_last + m_prev - m_new)
    m_st[...] = jnp.broadcast_to(m_new, m_st.shape)
    lane = lax.broadcasted_iota(jnp.int32, (1, PAIR), 1)
    halves = (lane < hd_w, lane >= hd_w)

    h_rows = []
    for p in range(nh // 2):
        sl = slice(p * PAIR, (p + 1) * PAIR)
        qp = q_ref[0, :, sl]
        kp = k_ref[0, :, sl]
        vt = vt_ref[0, 0, sl, :]
        inter_t = _dot_nt(c_st[p].astype(BF16), qp)
        qn = _dot_nt(n_st[p].astype(BF16), qp)
        for hh in range(2):
            hd = 2 * p + hh
            rows = slice(hh * hd_w, (hh + 1) * hd_w)
            log_d = jnp.where(visible, a_cols[:, hd:hd + 1] + b_rows[hd:hd + 1], NEG)
            m_t = jnp.maximum(log_inter[hd:hd + 1], jnp.max(log_d, axis=0, keepdims=True))
            w_inter = jnp.exp2(log_inter[hd:hd + 1] - m_t)
            q_h = jnp.where(halves[hh], qp, jnp.zeros_like(qp))
            s_t = _dot_nt(kp, q_h) * jnp.exp2(log_d - m_t)
            num = _dot(vt[rows], s_t.astype(BF16)) + w_inter * inter_t[rows]
            den = jnp.sum(s_t, axis=0, keepdims=True) + w_inter * qn[hh:hh + 1]
            h_t = num / jnp.maximum(jnp.abs(den), jnp.exp2(-m_t))
            ms = jnp.mean(h_t * h_t, axis=0, keepdims=True)
            h_rows.append(h_t * lax.rsqrt(ms + EPS))
        vw = jnp.concatenate([vt[hh * hd_w:(hh + 1) * hd_w].astype(F32) * w_keys[2 * p + hh:2 * p + hh + 1]
                              for hh in range(2)], axis=0).astype(BF16)
        upd = _dot(vw, kp)
        nk = _dot(w_keys.astype(BF16), kp)
        for hh in range(2):
            hd = 2 * p + hh
            rows = slice(hh * hd_w, (hh + 1) * hd_w)
            d = decay[hd:hd + 1]
            c_st[p, rows, :] = d * c_st[p, rows, :] + jnp.where(halves[hh], upd[rows], 0.0)
            n_st[p, hh:hh + 1, :] = d * n_st[p, hh:hh + 1, :] + jnp.where(halves[hh], nk[hd:hd + 1], 0.0)
    o_ref[0] = jnp.concatenate(h_rows, axis=0).T * nw_ref[...]


def _mlstm(qm, km, vmt, gates, head_nw, *, tl):
    bsz, s_len, w = qm.shape
    row = lambda width: pl.BlockSpec((1, tl, width), lambda b, s: (b, s, 0))
    n_pairs = MLSTM_HEADS // 2
    return pl.pallas_call(
        functools.partial(_mlstm_kernel, tl=tl),
        grid=(bsz, s_len // tl),
        in_specs=[row(w), row(w), pl.BlockSpec((1, 1, w, tl), lambda b, s: (b, s, 0, 0)),
                  row(LANES), _resident(head_nw.shape)],
        out_specs=row(w),
        out_shape=jax.ShapeDtypeStruct((bsz, s_len, w), F32),
        scratch_shapes=[pltpu.VMEM((n_pairs, PAIR, PAIR), F32),
                        pltpu.VMEM((n_pairs, SUBLANES, PAIR), F32),
                        pltpu.VMEM((MLSTM_HEADS, LANES), F32)],
        compiler_params=_params(2),
        name="mlstm",
    )(qm, km, vmt, gates, head_nw)


def _mix_out_kernel(x_ref, mod_ref, nw_ref, wc_ref, ya_ref, hm_ref, wba_ref, wbm_ref, wo_ref,
                    o_ref):
    mod = mod_ref[0]
    x = x_ref[0]
    hb = (_rms(x) * nw_ref[...] * (1.0 + mod[1:2]) + mod[0:1]).astype(BF16)
    o_gate = jax.nn.sigmoid(_dot(hb, wc_ref[:, C_O:C_O + MLSTM_W]))
    y_mlstm = (o_gate * hm_ref[0]).astype(BF16)
    merged = jax.nn.sigmoid(_dot(hb, wc_ref[:, C_GA:C_GA + D_MODEL])) * _dot(ya_ref[0], wba_ref[...])
    merged = merged + (jax.nn.sigmoid(_dot(hb, wc_ref[:, C_GM:C_GM + D_MODEL]))
                       * _dot(y_mlstm, wbm_ref[...]))
    o_ref[0] = x + mod[2:3] * _dot(merged.astype(BF16), wo_ref[...])


def _mix_out(x, mod, nw, w_c, y_mla, hm, wba, wbm, wo, *, tm):
    bsz, s_len, d = x.shape
    row = lambda w: pl.BlockSpec((1, tm, w), lambda b, s: (b, s, 0))
    return pl.pallas_call(
        _mix_out_kernel,
        grid=(bsz, s_len // tm),
        in_specs=[row(d), pl.BlockSpec((1, 6, d), lambda b, s: (b, 0, 0)),
                  _resident(nw.shape), _resident(w_c.shape), row(MLA_W), row(MLSTM_W),
                  _resident(wba.shape), _resident(wbm.shape), _resident(wo.shape)],
        out_specs=row(d),
        out_shape=jax.ShapeDtypeStruct((bsz, s_len, d), F32),
        compiler_params=_params(2),
        name="mix_out",
    )(x, mod, nw, w_c, y_mla, hm, wba, wbm, wo)


def _gelu_tanh(a):
    half = 0.5 * a
    t = jnp.tanh(a * (0.7978845608028654 + (0.7978845608028654 * 0.044715) * (a * a)))
    return half + half * t


def _ffn_kernel(x_ref, mod_ref, nw_ref, wup_ref, cw_ref, cb_ref, wdn_ref, fnw_ref, o_ref,
                ubuf, halo_ref, acc_ref, *, tm, final_norm):
    mod = mod_ref[0]
    x = x_ref[0]
    hb = (_rms(x) * nw_ref[...] * (1.0 + mod[4:5]) + mod[3:4]).astype(BF16)

    @pl.when(pl.program_id(1) == 0)
    def _():
        halo_ref[...] = jnp.zeros(halo_ref.shape, F32)

    def up(c):
        ubuf[c % 2] = _dot(hb, wup_ref[c])

    def gated_chunk(c):
        u = ubuf[c % 2]
        prev = halo_ref[c]
        halo_ref[c] = u[tm - SUBLANES:tm, :]
        u = _causal_conv(u, prev, [cw_ref[c, j:j + 1, :] for j in range(FFN_CONV)], cb_ref[c])
        return (_gelu_tanh(u[:, :FF_CHUNK]) * u[:, FF_CHUNK:]).astype(BF16)

    up(0)
    for c in range(N_FF):
        if c + 1 < N_FF:
            up(c + 1)
        part = _dot(gated_chunk(c), wdn_ref[c])
        if c == 0:
            acc_ref[...] = part
        else:
            acc_ref[...] += part
    y = x + mod[5:6] * acc_ref[...]
    if final_norm:
        y = _rms(y) * fnw_ref[...]
    o_ref[0] = y


def _ffn(x, mod, nw, wup, cw, cb, wdn, fnw, *, tm, final_norm):
    bsz, s_len, d = x.shape
    row = lambda w: pl.BlockSpec((1, tm, w), lambda b, s: (b, s, 0))
    return pl.pallas_call(
        functools.partial(_ffn_kernel, tm=tm, final_norm=final_norm),
        grid=(bsz, s_len // tm),
        in_specs=[row(d), pl.BlockSpec((1, 6, d), lambda b, s: (b, 0, 0)),
                  _resident(nw.shape), _resident(wup.shape), _resident(cw.shape),
                  _resident(cb.shape), _resident(wdn.shape), _resident(fnw.shape)],
        out_specs=row(d),
        out_shape=jax.ShapeDtypeStruct((bsz, s_len, d), F32),
        scratch_shapes=[pltpu.VMEM((2, tm, 2 * FF_CHUNK), F32),
                        pltpu.VMEM((N_FF, SUBLANES, 2 * FF_CHUNK), F32),
                        pltpu.VMEM((tm, d), F32)],
        compiler_params=_params(2),
        name="conv_ffn",
    )(x, mod, nw, wup, cw, cb, wdn, fnw)


def _rot_cols(w):
    half = w.shape[-1] // 2
    return jnp.concatenate([-w[..., half:], w[..., :half]], axis=-1)


def _pad_cols(w, before, total):
    pad = [(0, 0)] * (w.ndim - 1) + [(before, total - before - w.shape[-1])]
    return jnp.pad(w, pad)


def _layer_weights(w_in, w_uq, w_ukv, gate_b):
    d = w_in.shape[0]
    o = 0
    seg = {}
    for name, width in (("cq", Q_LORA), ("ckv", KV_LORA), ("kr", QK_ROPE), ("qm", MLSTM_W),
                        ("km", MLSTM_W), ("vm", MLSTM_W), ("om", MLSTM_W), ("im", MLSTM_HEADS),
                        ("fm", MLSTM_HEADS), ("ga", D_MODEL), ("gm", D_MODEL)):
        seg[name] = w_in[:, o:o + width]
        o += width
    gates = _pad_cols(jnp.concatenate([seg["im"], seg["fm"]], axis=1), 0, LANES)
    w_a = jnp.concatenate([seg["cq"], seg["ckv"],
                           _pad_cols(seg["kr"], QK_NOPE, LANES),
                           _pad_cols(_rot_cols(seg["kr"]), QK_NOPE, LANES),
                           seg["qm"], seg["km"], gates], axis=1).astype(BF16)
    wvm = seg["vm"].T.astype(BF16)
    w_c = jnp.concatenate([seg["om"], seg["ga"], seg["gm"]], axis=1).astype(BF16)
    uq = w_uq.reshape(Q_LORA, MLA_HEADS, QK_NOPE + QK_ROPE)
    wuqa = _pad_cols(uq, 0, HEAD_SLAB).reshape(Q_LORA, -1).astype(BF16)
    wuqb = _pad_cols(_rot_cols(uq[..., QK_NOPE:]), QK_NOPE, HEAD_SLAB).reshape(Q_LORA, -1).astype(BF16)
    ukv = w_ukv.reshape(KV_LORA, MLA_HEADS, QK_NOPE + V_HEAD)
    wuk = _pad_cols(ukv[..., :QK_NOPE], 0, HEAD_SLAB).reshape(KV_LORA, -1).astype(BF16)
    wuv = _pad_cols(ukv[..., QK_NOPE:], 0, HEAD_SLAB).reshape(KV_LORA, -1).T.astype(BF16)
    gb = _pad_cols(gate_b.reshape(1, -1), 0, LANES)
    assert w_a.shape == (d, A_COLS) and w_c.shape == (d, C_COLS)
    return w_a, w_c, wuqa, wuqb, wuk, wuv, wvm, gb


def _ffn_weights(w_up, conv_w, conv_b, w_down):
    def chunked(a):
        lead = a.shape[:-1]
        a = a.reshape(lead + (2, N_FF, FF_CHUNK))
        a = jnp.moveaxis(a, -2, 0)
        return a.reshape((N_FF,) + lead + (2 * FF_CHUNK,))
    return (chunked(w_up).astype(BF16), chunked(conv_w), chunked(conv_b.reshape(1, -1)),
            w_down.reshape(N_FF, FF_CHUNK, -1).astype(BF16))


def kernel(x, c, positions, ada_w, ada_b, norm_mix_w, w_in, q_norm_w, kv_norm_w, w_uq, w_ukv,
           mlstm_conv_w, mlstm_conv_b, mlstm_gate_b, mlstm_head_norm_w, w_br_mla, w_br_mlstm,
           w_out, norm_ffn_w, ffn_w_up, ffn_conv_w, ffn_conv_b, ffn_w_down, final_norm_w):
    bsz, s_len, d = x.shape
    depth = ada_w.shape[0]
    tm = min(512, s_len)
    tq = min(512, s_len)
    tl = min(256, s_len)

    mod_all = _modulation(c, ada_w, ada_b).reshape(depth, bsz, 6, d)
    cos, sin = _rope_tables(positions)
    ones = jnp.ones((bsz, s_len, QK_NOPE), F32)
    zeros = jnp.zeros((bsz, s_len, HEAD_SLAB - QK_NOPE - QK_ROPE), F32)
    cosp = jnp.concatenate([ones, cos, cos, zeros], axis=-1)
    sinp = jnp.concatenate([jnp.zeros_like(ones), sin, sin, zeros], axis=-1)
    row = lambda a: a.reshape(1, -1)

    for l in range(depth):
        mod = mod_all[l]
        w_a, w_c, wuqa, wuqb, wuk, wuv, wvm, gb = _layer_weights(w_in[l], w_uq[l], w_ukv[l],
                                                                mlstm_gate_b[l])
        q, k, v, qm, km, vm, gates = _in_proj(
            x, mod, row(norm_mix_w[l]), w_a, row(q_norm_w[l]), row(kv_norm_w[l]), wuqa, wuqb,
            wuk, wuv, wvm, mlstm_conv_w[l], row(mlstm_conv_b[l]), gb, cosp, sinp, tm=tm, tl=tl)
        y_mla = _attention(q, k, v, tq=tq)
        hm = _mlstm(qm, km, vm, gates, row(mlstm_head_norm_w[l]), tl=tl)
        x = _mix_out(x, mod, row(norm_mix_w[l]), w_c, y_mla, hm, w_br_mla[l].astype(BF16),
                     w_br_mlstm[l].astype(BF16), w_out[l].astype(BF16), tm=tm)
        wup, cw, cb, wdn = _ffn_weights(ffn_w_up[l], ffn_conv_w[l], ffn_conv_b[l], ffn_w_down[l])
        x = _ffn(x, mod, row(norm_ffn_w[l]), wup, cw, cb, wdn, row(final_norm_w), tm=tm,
                 final_norm=(l == depth - 1))
    return x
```

```python
import functools

import jax
import jax.numpy as jnp
from jax import lax
from jax.experimental import pallas as pl
from jax.experimental.pallas import tpu as pltpu

F32 = jnp.float32
BF16 = jnp.bfloat16

D_MODEL = 1024
CHUNK = 64
MLA_HEADS = 8
Q_LORA = 384
KV_LORA = 256
QK_NOPE = 64
QK_ROPE = 32
V_HEAD = 64
ROPE_THETA = 10000.0
MLSTM_HEADS = 8
MLSTM_HEAD_DIM = 64
MLSTM_W = MLSTM_HEADS * MLSTM_HEAD_DIM
MLSTM_CONV = 4
D_FF = 2816
FFN_CONV = 3
MLA_W = MLA_HEADS * V_HEAD
EPS = 1e-6

LANES = 128
SUBLANES = 8
HEAD_SLAB = 128
PAIR = 2 * MLSTM_HEAD_DIM
NEG = -1e30
LOG2_E = 1.4426950408889634

A_CQ = 0
A_CKV = A_CQ + Q_LORA
A_KR = A_CKV + KV_LORA
A_KRR = A_KR + LANES
A_QK = A_KRR + LANES
A_G = A_QK + 2 * MLSTM_W
A_COLS = A_G + LANES
C_O = 0
C_GA = C_O + MLSTM_W
C_GM = C_GA + D_MODEL
C_COLS = C_GM + D_MODEL

FF_CHUNK = 256
N_FF = D_FF // FF_CHUNK
FF_AHEAD = 3
ATTN_HEADS = 4

VMEM_LIMIT = 56 * 1024 * 1024


def _params(n_axes):
    return pltpu.CompilerParams(dimension_semantics=("arbitrary",) * n_axes,
                                vmem_limit_bytes=VMEM_LIMIT)


def _resident(shape):
    nd = len(shape)
    return pl.BlockSpec(shape, lambda *_: (0,) * nd, pipeline_mode=pl.Buffered(1))


def _rms(x):
    return x * lax.rsqrt(jnp.mean(x * x, axis=-1, keepdims=True) + EPS)


def _dot(a, b):
    return jnp.dot(a, b, preferred_element_type=F32)


def _dot_nt(a, b):
    return lax.dot_general(a, b, (((1,), (1,)), ((), ())), preferred_element_type=F32)


def _causal_conv(u, prev, taps, bias):
    width = len(taps)
    first = lax.broadcasted_iota(jnp.int32, (SUBLANES, 1), 0)
    y = bias + taps[width - 1] * u
    for j in range(1, width):
        shifted = pltpu.roll(u, j, axis=0)
        head = jnp.where(first < j, pltpu.roll(prev, j, axis=0), shifted[0:SUBLANES])
        shifted = jnp.concatenate([head, shifted[SUBLANES:]], axis=0)
        y = y + taps[width - 1 - j] * shifted
    return y


def _mod_kernel(c_ref, w_ref, b_ref, o_ref):
    c = c_ref[...]
    c_act = (c * jax.nn.sigmoid(c)).astype(BF16)
    o_ref[0] = _dot(c_act, w_ref[0].astype(BF16)) + b_ref[0]


def _modulation(c, ada_w, ada_b):
    depth, d, n = ada_w.shape
    bsz = c.shape[0]
    tn = 1024
    return pl.pallas_call(
        _mod_kernel,
        grid=(depth, n // tn),
        in_specs=[pl.BlockSpec((bsz, d), lambda l, j: (0, 0)),
                  pl.BlockSpec((1, d, tn), lambda l, j: (l, 0, j)),
                  pl.BlockSpec((1, 1, tn), lambda l, j: (l, 0, j))],
        out_specs=pl.BlockSpec((1, bsz, tn), lambda l, j: (l, 0, j)),
        out_shape=jax.ShapeDtypeStruct((depth, bsz, n), F32),
        compiler_params=_params(2),
        name="modulation",
    )(c, ada_w, ada_b.reshape(depth, 1, n))


def _rope_kernel(pos_ref, inv_ref, cos_ref, sin_ref):
    ang = pos_ref[...].astype(F32) * inv_ref[...]
    cos_ref[...] = jnp.cos(ang)
    sin_ref[...] = jnp.sin(ang)


def _rope_tables(positions):
    bsz, s_len = positions.shape
    half = QK_ROPE // 2
    per_row = LANES // half
    rows = bsz * s_len // per_row
    pos = jnp.repeat(positions.reshape(rows, per_row), half, axis=1)
    inv = ROPE_THETA ** (-jnp.arange(half, dtype=F32) / half)
    inv = jnp.tile(inv, per_row).reshape(1, LANES)
    tr = min(rows, 512)
    cos, sin = pl.pallas_call(
        _rope_kernel,
        grid=(rows // tr,),
        in_specs=[pl.BlockSpec((tr, LANES), lambda i: (i, 0)),
                  pl.BlockSpec((1, LANES), lambda i: (0, 0))],
        out_specs=[pl.BlockSpec((tr, LANES), lambda i: (i, 0))] * 2,
        out_shape=[jax.ShapeDtypeStruct((rows, LANES), F32)] * 2,
        compiler_params=_params(1),
        name="rope_tables",
    )(pos, inv)
    return cos.reshape(bsz, s_len, half), sin.reshape(bsz, s_len, half)


def _in_proj_kernel(x_ref, mod_ref, nw_ref, win_ref, qnw_ref, kvnw_ref, wuqa_ref, wuqb_ref,
                    wuk_ref, wuv_ref, wvm_ref, cw_ref, cb_ref, gb_ref, cos_ref, sin_ref,
                    q_ref, k_ref, v_ref, qm_ref, km_ref, vm_ref, g_ref, halo_ref, qk_buf, *, tm, tl):
    mod = mod_ref[0]
    h = _rms(x_ref[0]) * nw_ref[...] * (1.0 + mod[1:2]) + mod[0:1]
    hb = h.astype(BF16)
    cos = cos_ref[0]
    sin = sin_ref[0]

    def proj(lo, width):
        return _dot(hb, win_ref[:, lo:lo + width])

    @pl.when(pl.program_id(1) == 0)
    def _():
        halo_ref[...] = jnp.zeros(halo_ref.shape, F32)

    cq = proj(A_CQ, Q_LORA)
    ckv = proj(A_CKV, KV_LORA)
    kr = proj(A_KR, LANES) * cos + proj(A_KRR, LANES) * sin
    g = proj(A_G, LANES) + gb_ref[...]
    qk_buf[...] = proj(A_QK, 2 * MLSTM_W)
    vmt = _dot_nt(wvm_ref[...], hb).astype(BF16)
    for c in range(tm // tl):
        vm_ref[0, c] = vmt[:, c * tl:(c + 1) * tl]

    cqn = (_rms(cq) * qnw_ref[...]).astype(BF16)
    ckvn = (_rms(ckv) * kvnw_ref[...]).astype(BF16)

    logf = jnp.minimum(g, 0.0) - jnp.log1p(jnp.exp(-jnp.abs(g)))
    lane = lax.broadcasted_iota(jnp.int32, g.shape, 1)
    g_ref[0] = jnp.where(lane < MLSTM_HEADS, g, logf)

    slab_row = lax.broadcasted_iota(jnp.int32, (MLA_HEADS * HEAD_SLAB, 1), 0) % HEAD_SLAB
    v_ref[0, 0] = jnp.where(slab_row == V_HEAD, 1.0, _dot_nt(wuv_ref[...], ckvn)).astype(BF16)
    two = 2 * HEAD_SLAB
    for hp in range(MLA_HEADS // 2):
        kn = _dot(ckvn, wuk_ref[:, hp * two:(hp + 1) * two])
        for hh in range(2):
            sl = slice(hh * HEAD_SLAB, (hh + 1) * HEAD_SLAB)
            k_ref[0, :, hp * two + hh * HEAD_SLAB:hp * two + (hh + 1) * HEAD_SLAB] = \
                (kn[:, sl] + kr).astype(BF16)

    prev = halo_ref[...]
    halo_ref[...] = qk_buf[tm - SUBLANES:tm, :]

    scale = (QK_NOPE + QK_ROPE) ** -0.5 * LOG2_E
    n_groups = MLA_HEADS // 2
    gw = 2 * MLSTM_W // n_groups
    for hp in range(n_groups):
        qa = _dot(cqn, wuqa_ref[:, hp * two:(hp + 1) * two])
        qb = _dot(cqn, wuqb_ref[:, hp * two:(hp + 1) * two])
        for hh in range(2):
            sl = slice(hh * HEAD_SLAB, (hh + 1) * HEAD_SLAB)
            q_ref[0, :, hp * two + hh * HEAD_SLAB:hp * two + (hh + 1) * HEAD_SLAB] = \
                ((qa[:, sl] * cos + qb[:, sl] * sin) * scale).astype(BF16)
        cols = slice(hp * gw, (hp + 1) * gw)
        y = _causal_conv(qk_buf[:, cols], prev[:, cols],
                         [cw_ref[j:j + 1, cols] for j in range(MLSTM_CONV)], cb_ref[:, cols])
        y = y * jax.nn.sigmoid(y)
        if hp < n_groups // 2:
            qm_ref[0, :, cols] = y.astype(BF16)
        else:
            km_ref[0, :, hp * gw - MLSTM_W:(hp + 1) * gw - MLSTM_W] = \
                (y * (MLSTM_HEAD_DIM ** -0.5)).astype(BF16)


def _in_proj(x, mod, nw, w_a, qnw, kvnw, wuqa, wuqb, wuk, wuv, wvm, cw, cb, gb, cosp, sinp, *, tm, tl):
    bsz, s_len, d = x.shape
    row = lambda w: pl.BlockSpec((1, tm, w), lambda b, s: (b, s, 0))
    slabs = MLA_HEADS * HEAD_SLAB
    out_widths = (slabs, slabs, LANES, MLSTM_W, MLSTM_W, LANES, LANES)
    out_shape = [jax.ShapeDtypeStruct((bsz, s_len, w), BF16) for w in out_widths]
    out_specs = [row(w) for w in out_widths]
    out_shape[2] = jax.ShapeDtypeStruct((bsz, s_len // tm, slabs, tm), BF16)
    out_specs[2] = pl.BlockSpec((1, 1, slabs, tm), lambda b, s: (b, s, 0, 0))
    out_shape[5] = jax.ShapeDtypeStruct((bsz, s_len // tl, MLSTM_W, tl), BF16)
    out_specs[5] = pl.BlockSpec((1, tm // tl, MLSTM_W, tl), lambda b, s: (b, s, 0, 0))
    out_shape[6] = jax.ShapeDtypeStruct((bsz, s_len, LANES), F32)
    out_specs[6] = row(LANES)
    return pl.pallas_call(
        functools.partial(_in_proj_kernel, tm=tm, tl=tl),
        grid=(bsz, s_len // tm),
        in_specs=[row(d),
                  pl.BlockSpec((1, 6, d), lambda b, s: (b, 0, 0)),
                  _resident(nw.shape), _resident(w_a.shape), _resident(qnw.shape),
                  _resident(kvnw.shape), _resident(wuqa.shape), _resident(wuqb.shape),
                  _resident(wuk.shape), _resident(wuv.shape), _resident(wvm.shape),
                  _resident(cw.shape),
                  _resident(cb.shape), _resident(gb.shape), row(LANES), row(LANES)],
        out_specs=out_specs,
        out_shape=out_shape,
        scratch_shapes=[pltpu.VMEM((SUBLANES, 2 * MLSTM_W), F32),
                        pltpu.VMEM((tm, 2 * MLSTM_W), F32)],
        compiler_params=_params(2),
        name="in_proj",
    )(x, mod, nw, w_a, qnw, kvnw, wuqa, wuqb, wuk, wuv, wvm, cw, cb, gb, cosp, sinp)


def _attn_kernel(q_ref, k_ref, vt_ref, o_ref, *scratch, tq):
    nh = ATTN_HEADS
    qi = pl.program_id(2)
    key_chunk = lax.broadcasted_iota(jnp.int32, (tq, 1), 0) // CHUNK
    qry_chunk = lax.broadcasted_iota(jnp.int32, (1, tq), 1) // CHUNK
    visible = key_chunk <= qry_chunk
    heads = [slice(hh * HEAD_SLAB, (hh + 1) * HEAD_SLAB) for hh in range(nh)]
    s_buf, mx_buf, m_ref, acc_ref = (scratch[i::4] for i in range(4))

    def scores(j, hh, diagonal):
        start = pl.multiple_of(j * tq, tq)
        st = _dot_nt(k_ref[0, pl.ds(start, tq), heads[hh]], q_ref[0, :, heads[hh]])
        if diagonal:
            st = jnp.where(visible, st, NEG)
        s_buf[hh][...] = st
        mx_buf[hh][...] = jnp.max(st, axis=0, keepdims=True)

    def accumulate(j, hh):
        m_old = m_ref[hh][...]
        m_new = jnp.maximum(m_old, mx_buf[hh][...])
        p = jnp.exp2(s_buf[hh][...] - m_new).astype(BF16)
        acc_ref[hh][...] = (jnp.exp2(m_old - m_new) * acc_ref[hh][...]
                            + _dot(vt_ref[0, j, heads[hh], :], p))
        m_ref[hh][...] = m_new

    def block(j, diagonal, next_diagonal):
        for hh in range(nh):
            if hh + 1 < nh:
                scores(j, hh + 1, diagonal)
            elif next_diagonal is not None:
                scores(j + 1, 0, next_diagonal)
            accumulate(j, hh)

    for hh in range(nh):
        m_ref[hh][...] = jnp.full(m_ref[hh].shape, NEG, F32)
        acc_ref[hh][...] = jnp.zeros(acc_ref[hh].shape, F32)

    @pl.when(qi == 0)
    def _():
        scores(0, 0, True)

    @pl.when(qi > 0)
    def _():
        scores(0, 0, False)

        def body(j, carry):
            block(j, False, False)
            return carry

        lax.fori_loop(0, qi - 1, body, 0)
        block(qi - 1, False, True)

    block(qi, True, None)
    outs = [acc_ref[hh][0:V_HEAD, :] / acc_ref[hh][V_HEAD:V_HEAD + 1, :] for hh in range(nh)]
    o_ref[0] = jnp.concatenate(outs, axis=0).T.astype(BF16)


def _attention(q, k, vt, *, tq):
    bsz, s_len, _ = q.shape
    width = ATTN_HEADS * HEAD_SLAB
    n_kv = s_len // tq
    return pl.pallas_call(
        functools.partial(_attn_kernel, tq=tq),
        grid=(bsz, MLA_HEADS // ATTN_HEADS, n_kv),
        in_specs=[pl.BlockSpec((1, tq, width), lambda b, h, i: (b, i, h)),
                  pl.BlockSpec((1, s_len, width), lambda b, h, i: (b, 0, h)),
                  pl.BlockSpec((1, n_kv, width, tq), lambda b, h, i: (b, 0, h, 0))],
        out_specs=pl.BlockSpec((1, tq, ATTN_HEADS * V_HEAD), lambda b, h, i: (b, i, h)),
        out_shape=jax.ShapeDtypeStruct((bsz, s_len, MLA_W), BF16),
        scratch_shapes=[pltpu.VMEM((tq, tq), F32), pltpu.VMEM((1, tq), F32),
                        pltpu.VMEM((1, tq), F32), pltpu.VMEM((HEAD_SLAB, tq), F32)] * ATTN_HEADS,
        compiler_params=_params(3),
        name="mla_attention",
    )(q, k, vt)


def _split3(a):
    hi = a.astype(BF16)
    r = a - hi.astype(F32)
    mid = r.astype(BF16)
    lo = (r - mid.astype(F32)).astype(BF16)
    return hi, mid, lo


def _mlstm_kernel(q_ref, k_ref, vt_ref, g_ref, nw_ref, o_ref, m_st, *state, tl):
    c_st, n_st = state[0::2], state[1::2]

    @pl.when(pl.program_id(1) == 0)
    def _():
        for ref in state + (m_st,):
            ref[...] = jnp.zeros(ref.shape, F32)

    nh, hd_w = MLSTM_HEADS, MLSTM_HEAD_DIM
    g = g_ref[0] * LOG2_E
    key = lax.broadcasted_iota(jnp.int32, (tl, 1), 0)
    qry = lax.broadcasted_iota(jnp.int32, (1, tl), 1)
    visible = key <= qry
    tri = jnp.where(qry <= key, 1.0, 0.0).astype(BF16)
    b_all = sum(_dot(tri, part) for part in _split3(g))
    a_cols = g - pltpu.roll(b_all, LANES - nh, axis=1)
    i_rows = g.T[0:nh]
    b_rows = b_all.T[nh:2 * nh]
    a_rows = i_rows - b_rows
    m_prev = m_st[:, 0:1]
    b_last = b_rows[:, tl - 1:tl]
    log_inter = b_rows + m_prev
    log_w = b_last + a_rows
    m_new = jnp.maximum(b_last + m_prev, jnp.max(log_w, axis=1, keepdims=True))
    w_keys = jnp.exp2(log_w - m_new)
    decay = jnp.exp2(b_last + m_prev - m_new)
    m_st[...] = jnp.broadcast_to(m_new, m_st.shape)
    lane = lax.broadcasted_iota(jnp.int32, (1, PAIR), 1)
    halves = (lane < hd_w, lane >= hd_w)

    h_rows = []
    for p in range(nh // 2):
        sl = slice(p * PAIR, (p + 1) * PAIR)
        qp = q_ref[0, :, sl]
        kp = k_ref[0, :, sl]
        vt = vt_ref[0, 0, sl, :]
        inter_t = _dot_nt(c_st[p][...].astype(BF16), qp)
        qn = _dot_nt(n_st[p][...].astype(BF16), qp)
        for hh in range(2):
            hd = 2 * p + hh
            rows = slice(hh * hd_w, (hh + 1) * hd_w)
            log_d = jnp.where(visible, a_cols[:, hd:hd + 1] + b_rows[hd:hd + 1], NEG)
            m_t = jnp.maximum(log_inter[hd:hd + 1], jnp.max(log_d, axis=0, keepdims=True))
            w_inter = jnp.exp2(log_inter[hd:hd + 1] - m_t)
            q_h = jnp.where(halves[hh], qp, jnp.zeros_like(qp))
            s_t = _dot_nt(kp, q_h) * jnp.exp2(log_d - m_t)
            num = _dot(vt[rows], s_t.astype(BF16)) + w_inter * inter_t[rows]
            den = jnp.sum(s_t, axis=0, keepdims=True) + w_inter * qn[hh:hh + 1]
            h_t = num / jnp.maximum(jnp.abs(den), jnp.exp2(-m_t))
            ms = jnp.mean(h_t * h_t, axis=0, keepdims=True)
            h_rows.append(h_t * lax.rsqrt(ms + EPS))
        vw = jnp.concatenate([vt[hh * hd_w:(hh + 1) * hd_w].astype(F32) * w_keys[2 * p + hh:2 * p + hh + 1]
                              for hh in range(2)], axis=0).astype(BF16)
        upd = _dot(vw, kp)
        nk = _dot(w_keys.astype(BF16), kp)
        for hh in range(2):
            hd = 2 * p + hh
            rows = slice(hh * hd_w, (hh + 1) * hd_w)
            d = decay[hd:hd + 1]
            c_st[p][rows, :] = d * c_st[p][rows, :] + jnp.where(halves[hh], upd[rows], 0.0)
            n_st[p][hh:hh + 1, :] = (d * n_st[p][hh:hh + 1, :]
                                     + jnp.where(halves[hh], nk[hd:hd + 1], 0.0))
    o_ref[0] = jnp.concatenate(h_rows, axis=0).T * nw_ref[...]


def _mlstm(qm, km, vmt, gates, head_nw, *, tl):
    bsz, s_len, w = qm.shape
    row = lambda width: pl.BlockSpec((1, tl, width), lambda b, s: (b, s, 0))
    n_pairs = MLSTM_HEADS // 2
    return pl.pallas_call(
        functools.partial(_mlstm_kernel, tl=tl),
        grid=(bsz, s_len // tl),
        in_specs=[row(w), row(w), pl.BlockSpec((1, 1, w, tl), lambda b, s: (b, s, 0, 0)),
                  row(LANES), _resident(head_nw.shape)],
        out_specs=row(w),
        out_shape=jax.ShapeDtypeStruct((bsz, s_len, w), F32),
        scratch_shapes=[pltpu.VMEM((MLSTM_HEADS, LANES), F32)]
        + [pltpu.VMEM((PAIR, PAIR), F32), pltpu.VMEM((SUBLANES, PAIR), F32)] * n_pairs,
        compiler_params=_params(2),
        name="mlstm",
    )(qm, km, vmt, gates, head_nw)


def _mix_out_kernel(x_ref, mod_ref, nw_ref, wc_ref, ya_ref, hm_ref, wba_ref, wbm_ref, wo_ref,
                    o_ref):
    mod = mod_ref[0]
    x = x_ref[0]
    hb = (_rms(x) * nw_ref[...] * (1.0 + mod[1:2]) + mod[0:1]).astype(BF16)
    o_gate = jax.nn.sigmoid(_dot(hb, wc_ref[:, C_O:C_O + MLSTM_W]))
    y_mlstm = (o_gate * hm_ref[0]).astype(BF16)
    merged = jax.nn.sigmoid(_dot(hb, wc_ref[:, C_GA:C_GA + D_MODEL])) * _dot(ya_ref[0], wba_ref[...])
    merged = merged + (jax.nn.sigmoid(_dot(hb, wc_ref[:, C_GM:C_GM + D_MODEL]))
                       * _dot(y_mlstm, wbm_ref[...]))
    o_ref[0] = x + mod[2:3] * _dot(merged.astype(BF16), wo_ref[...])


def _mix_out(x, mod, nw, w_c, y_mla, hm, wba, wbm, wo, *, tm):
    bsz, s_len, d = x.shape
    row = lambda w: pl.BlockSpec((1, tm, w), lambda b, s: (b, s, 0))
    return pl.pallas_call(
        _mix_out_kernel,
        grid=(bsz, s_len // tm),
        in_specs=[row(d), pl.BlockSpec((1, 6, d), lambda b, s: (b, 0, 0)),
                  _resident(nw.shape), _resident(w_c.shape), row(MLA_W), row(MLSTM_W),
                  _resident(wba.shape), _resident(wbm.shape), _resident(wo.shape)],
        out_specs=row(d),
        out_shape=jax.ShapeDtypeStruct((bsz, s_len, d), F32),
        compiler_params=_params(2),
        name="mix_out",
    )(x, mod, nw, w_c, y_mla, hm, wba, wbm, wo)


def _gelu_tanh(a):
    k1 = -2.0 * 0.7978845608028654 * LOG2_E
    k2 = k1 * 0.044715
    return a / (1.0 + jnp.exp2(a * (k1 + k2 * (a * a))))


def _ffn_kernel(x_ref, mod_ref, nw_ref, wup_ref, cw_ref, cb_ref, wdn_ref, fnw_ref, o_ref,
                halo_ref, acc_ref, *ubuf, tm, final_norm):
    mod = mod_ref[0]
    x = x_ref[0]
    hb = (_rms(x) * nw_ref[...] * (1.0 + mod[4:5]) + mod[3:4]).astype(BF16)

    @pl.when(pl.program_id(1) == 0)
    def _():
        halo_ref[...] = jnp.zeros(halo_ref.shape, F32)

    def cols(c, half):
        lo = half * D_FF + c * FF_CHUNK
        return slice(lo, lo + FF_CHUNK)

    def up(c):
        for half in range(2):
            ubuf[2 * (c % FF_AHEAD) + half][...] = _dot(hb, wup_ref[:, cols(c, half)])

    def conv_half(c, half):
        sl = cols(c, half)
        u = ubuf[2 * (c % FF_AHEAD) + half][...]
        prev = halo_ref[:, sl]
        halo_ref[:, sl] = u[tm - SUBLANES:tm, :]
        return _causal_conv(u, prev, [cw_ref[j:j + 1, sl] for j in range(FFN_CONV)], cb_ref[:, sl])

    def down(c):
        gated = (_gelu_tanh(conv_half(c, 0)) * conv_half(c, 1)).astype(BF16)
        return _dot(gated, wdn_ref[c * FF_CHUNK:(c + 1) * FF_CHUNK, :])

    for c in range(FF_AHEAD - 1):
        up(c)
    pending = None
    for c in range(N_FF):
        if c + FF_AHEAD - 1 < N_FF:
            up(c + FF_AHEAD - 1)
        part = down(c)
        if c % 2 == 0 and c + 1 < N_FF:
            pending = part
            continue
        if pending is not None:
            part = pending + part
            pending = None
        if c <= 1:
            acc_ref[...] = part
        else:
            acc_ref[...] += part
    y = x + mod[5:6] * acc_ref[...]
    if final_norm:
        y = _rms(y) * fnw_ref[...]
    o_ref[0] = y


def _ffn(x, mod, nw, wup, cw, cb, wdn, fnw, *, tm, final_norm):
    bsz, s_len, d = x.shape
    row = lambda w: pl.BlockSpec((1, tm, w), lambda b, s: (b, s, 0))
    return pl.pallas_call(
        functools.partial(_ffn_kernel, tm=tm, final_norm=final_norm),
        grid=(bsz, s_len // tm),
        in_specs=[row(d), pl.BlockSpec((1, 6, d), lambda b, s: (b, 0, 0)),
                  _resident(nw.shape), _resident(wup.shape), _resident(cw.shape),
                  _resident(cb.shape), _resident(wdn.shape), _resident(fnw.shape)],
        out_specs=row(d),
        out_shape=jax.ShapeDtypeStruct((bsz, s_len, d), F32),
        scratch_shapes=[pltpu.VMEM((SUBLANES, 2 * D_FF), F32), pltpu.VMEM((tm, d), F32)]
        + [pltpu.VMEM((tm, FF_CHUNK), F32)] * (2 * FF_AHEAD),
        compiler_params=_params(2),
        name="conv_ffn",
    )(x, mod, nw, wup, cw, cb, wdn, fnw)


def _rot_cols(w):
    half = w.shape[-1] // 2
    return jnp.concatenate([-w[..., half:], w[..., :half]], axis=-1)


def _pad_cols(w, before, total):
    pad = [(0, 0)] * (w.ndim - 1) + [(before, total - before - w.shape[-1])]
    return jnp.pad(w, pad)


def _layer_weights(w_in, w_uq, w_ukv, gate_b):
    d = w_in.shape[0]
    o = 0
    seg = {}
    for name, width in (("cq", Q_LORA), ("ckv", KV_LORA), ("kr", QK_ROPE), ("qm", MLSTM_W),
                        ("km", MLSTM_W), ("vm", MLSTM_W), ("om", MLSTM_W), ("im", MLSTM_HEADS),
                        ("fm", MLSTM_HEADS), ("ga", D_MODEL), ("gm", D_MODEL)):
        seg[name] = w_in[:, o:o + width]
        o += width
    gates = _pad_cols(jnp.concatenate([seg["im"], seg["fm"]], axis=1), 0, LANES)
    w_a = jnp.concatenate([seg["cq"], seg["ckv"],
                           _pad_cols(seg["kr"], QK_NOPE, LANES),
                           _pad_cols(_rot_cols(seg["kr"]), QK_NOPE, LANES),
                           seg["qm"], seg["km"], gates], axis=1).astype(BF16)
    wvm = seg["vm"].T.astype(BF16)
    w_c = jnp.concatenate([seg["om"], seg["ga"], seg["gm"]], axis=1).astype(BF16)
    uq = w_uq.reshape(Q_LORA, MLA_HEADS, QK_NOPE + QK_ROPE)
    wuqa = _pad_cols(uq, 0, HEAD_SLAB).reshape(Q_LORA, -1).astype(BF16)
    wuqb = _pad_cols(_rot_cols(uq[..., QK_NOPE:]), QK_NOPE, HEAD_SLAB).reshape(Q_LORA, -1).astype(BF16)
    ukv = w_ukv.reshape(KV_LORA, MLA_HEADS, QK_NOPE + V_HEAD)
    wuk = _pad_cols(ukv[..., :QK_NOPE], 0, HEAD_SLAB).reshape(KV_LORA, -1).astype(BF16)
    wuv = _pad_cols(ukv[..., QK_NOPE:], 0, HEAD_SLAB).reshape(KV_LORA, -1).T.astype(BF16)
    gb = _pad_cols(gate_b.reshape(1, -1), 0, LANES)
    assert w_a.shape == (d, A_COLS) and w_c.shape == (d, C_COLS)
    return w_a, w_c, wuqa, wuqb, wuk, wuv, wvm, gb


def kernel(x, c, positions, ada_w, ada_b, norm_mix_w, w_in, q_norm_w, kv_norm_w, w_uq, w_ukv,
           mlstm_conv_w, mlstm_conv_b, mlstm_gate_b, mlstm_head_norm_w, w_br_mla, w_br_mlstm,
           w_out, norm_ffn_w, ffn_w_up, ffn_conv_w, ffn_conv_b, ffn_w_down, final_norm_w):
    bsz, s_len, d = x.shape
    depth = ada_w.shape[0]
    tm = min(512, s_len)
    tq = min(512, s_len)
    tl = min(256, s_len)

    mod_all = _modulation(c, ada_w, ada_b).reshape(depth, bsz, 6, d)
    cos, sin = _rope_tables(positions)
    ones = jnp.ones((bsz, s_len, QK_NOPE), F32)
    zeros = jnp.zeros((bsz, s_len, HEAD_SLAB - QK_NOPE - QK_ROPE), F32)
    cosp = jnp.concatenate([ones, cos, cos, zeros], axis=-1)
    sinp = jnp.concatenate([jnp.zeros_like(ones), sin, sin, zeros], axis=-1)
    row = lambda a: a.reshape(1, -1)

    for l in range(depth):
        mod = mod_all[l]
        w_a, w_c, wuqa, wuqb, wuk, wuv, wvm, gb = _layer_weights(w_in[l], w_uq[l], w_ukv[l],
                                                                mlstm_gate_b[l])
        q, k, v, qm, km, vm, gates = _in_proj(
            x, mod, row(norm_mix_w[l]), w_a, row(q_norm_w[l]), row(kv_norm_w[l]), wuqa, wuqb,
            wuk, wuv, wvm, mlstm_conv_w[l], row(mlstm_conv_b[l]), gb, cosp, sinp, tm=tm, tl=tl)
        y_mla = _attention(q, k, v, tq=tq)
        hm = _mlstm(qm, km, vm, gates, row(mlstm_head_norm_w[l]), tl=tl)
        x = _mix_out(x, mod, row(norm_mix_w[l]), w_c, y_mla, hm, w_br_mla[l].astype(BF16),
                     w_br_mlstm[l].astype(BF16), w_out[l].astype(BF16), tm=tm)
        x = _ffn(x, mod, row(norm_ffn_w[l]), ffn_w_up[l].astype(BF16), ffn_conv_w[l],
                 row(ffn_conv_b[l]), ffn_w_down[l].astype(BF16), row(final_norm_w), tm=tm,
                 final_norm=(l == depth - 1))
    return x
```

```python
import functools

import jax
import jax.numpy as jnp
from jax import lax
from jax.experimental import pallas as pl
from jax.experimental.pallas import tpu as pltpu

F32 = jnp.float32
BF16 = jnp.bfloat16

D_MODEL = 1024
CHUNK = 64
MLA_HEADS = 8
Q_LORA = 384
KV_LORA = 256
QK_NOPE = 64
QK_ROPE = 32
V_HEAD = 64
ROPE_THETA = 10000.0
MLSTM_HEADS = 8
MLSTM_HEAD_DIM = 64
MLSTM_W = MLSTM_HEADS * MLSTM_HEAD_DIM
MLSTM_CONV = 4
D_FF = 2816
FFN_CONV = 3
MLA_W = MLA_HEADS * V_HEAD
EPS = 1e-6

LANES = 128
SUBLANES = 8
HEAD_SLAB = 128
PAIR = 2 * MLSTM_HEAD_DIM
NEG = -1e30
LOG2_E = 1.4426950408889634

A_CQ = 0
A_CKV = A_CQ + Q_LORA
A_KR = A_CKV + KV_LORA
A_KRR = A_KR + LANES
A_QK = A_KRR + LANES
A_G = A_QK + 2 * MLSTM_W
A_COLS = A_G + LANES
C_O = 0
C_GA = C_O + MLSTM_W
C_GM = C_GA + D_MODEL
C_COLS = C_GM + D_MODEL

FF_CHUNK = 256
N_FF = D_FF // FF_CHUNK
FF_AHEAD = 3
ATTN_HEADS = 8

VMEM_LIMIT = 56 * 1024 * 1024


def _params(n_axes):
    return pltpu.CompilerParams(dimension_semantics=("arbitrary",) * n_axes,
                                vmem_limit_bytes=VMEM_LIMIT)


def _resident(shape):
    nd = len(shape)
    return pl.BlockSpec(shape, lambda *_: (0,) * nd, pipeline_mode=pl.Buffered(1))


def _rms(x):
    return x * lax.rsqrt(jnp.mean(x * x, axis=-1, keepdims=True) + EPS)


def _dot(a, b):
    return jnp.dot(a, b, preferred_element_type=F32)


def _dot_nt(a, b):
    return lax.dot_general(a, b, (((1,), (1,)), ((), ())), preferred_element_type=F32)


def _mod_kernel(c_ref, w_ref, b_ref, o_ref):
    c = c_ref[...]
    c_act = (c * jax.nn.sigmoid(c)).astype(BF16)
    o_ref[0] = _dot(c_act, w_ref[0].astype(BF16)) + b_ref[0]


def _modulation(c, ada_w, ada_b):
    depth, d, n = ada_w.shape
    bsz = c.shape[0]
    tn = 1024
    return pl.pallas_call(
        _mod_kernel,
        grid=(depth, n // tn),
        in_specs=[pl.BlockSpec((bsz, d), lambda l, j: (0, 0)),
                  pl.BlockSpec((1, d, tn), lambda l, j: (l, 0, j)),
                  pl.BlockSpec((1, 1, tn), lambda l, j: (l, 0, j))],
        out_specs=pl.BlockSpec((1, bsz, tn), lambda l, j: (l, 0, j)),
        out_shape=jax.ShapeDtypeStruct((depth, bsz, n), F32),
        compiler_params=_params(2),
        name="modulation",
    )(c, ada_w, ada_b.reshape(depth, 1, n))


def _rope_kernel(pos_ref, inv_ref, cos_ref, sin_ref):
    ang = pos_ref[...].astype(F32) * inv_ref[...]
    cos_ref[...] = jnp.cos(ang)
    sin_ref[...] = jnp.sin(ang)


def _rope_tables(positions):
    bsz, s_len = positions.shape
    half = QK_ROPE // 2
    per_row = LANES // half
    rows = bsz * s_len // per_row
    pos = jnp.repeat(positions.reshape(rows, per_row), half, axis=1)
    inv = ROPE_THETA ** (-jnp.arange(half, dtype=F32) / half)
    inv = jnp.tile(inv, per_row).reshape(1, LANES)
    tr = min(rows, 512)
    cos, sin = pl.pallas_call(
        _rope_kernel,
        grid=(rows // tr,),
        in_specs=[pl.BlockSpec((tr, LANES), lambda i: (i, 0)),
                  pl.BlockSpec((1, LANES), lambda i: (0, 0))],
        out_specs=[pl.BlockSpec((tr, LANES), lambda i: (i, 0))] * 2,
        out_shape=[jax.ShapeDtypeStruct((rows, LANES), F32)] * 2,
        compiler_params=_params(1),
        name="rope_tables",
    )(pos, inv)
    return cos.reshape(bsz, s_len, half), sin.reshape(bsz, s_len, half)


def _in_proj_kernel(x_ref, mod_ref, nw_ref, win_ref, qnw_ref, kvnw_ref, wuqa_ref, wuqb_ref,
                    wuk_ref, wuv_ref, wvm_ref, cw_ref, cb_ref, gb_ref, cos_ref, sin_ref,
                    q_ref, k_ref, v_ref, qm_ref, km_ref, vm_ref, g_ref, halo_ref, qk_buf, *, tm, tl):
    mod = mod_ref[0]
    h = _rms(x_ref[0]) * nw_ref[...] * (1.0 + mod[1:2]) + mod[0:1]
    hb = h.astype(BF16)
    cos = cos_ref[0]
    sin = sin_ref[0]

    def proj(lo, width):
        return _dot(hb, win_ref[:, lo:lo + width])

    @pl.when(pl.program_id(1) == 0)
    def _():
        halo_ref[...] = jnp.zeros(halo_ref.shape, F32)

    cq = proj(A_CQ, Q_LORA)
    ckv = proj(A_CKV, KV_LORA)
    kr = proj(A_KR, LANES) * cos + proj(A_KRR, LANES) * sin
    g = proj(A_G, LANES) + gb_ref[...]
    qk = proj(A_QK, 2 * MLSTM_W)
    top = SUBLANES
    for s in range(2 * MLSTM_W // LANES):
        lanes = slice(s * LANES, (s + 1) * LANES)
        qk_buf[s, 0:top, :] = halo_ref[:, lanes]
        qk_buf[s, top:top + tm, :] = qk[:, lanes]
    halo_ref[...] = qk[tm - SUBLANES:tm, :]
    vmt = _dot_nt(wvm_ref[...], hb).astype(BF16)
    for c in range(tm // tl):
        vm_ref[0, c] = vmt[:, c * tl:(c + 1) * tl]

    cqn = (_rms(cq) * qnw_ref[...]).astype(BF16)
    ckvn = (_rms(ckv) * kvnw_ref[...]).astype(BF16)

    logf = jnp.minimum(g, 0.0) - jnp.log1p(jnp.exp(-jnp.abs(g)))
    lane = lax.broadcasted_iota(jnp.int32, g.shape, 1)
    g_ref[0] = jnp.where(lane < MLSTM_HEADS, g, logf)

    slab_row = lax.broadcasted_iota(jnp.int32, (MLA_HEADS * HEAD_SLAB, 1), 0) % HEAD_SLAB
    v_ref[0, 0] = jnp.where(slab_row == V_HEAD, 1.0, _dot_nt(wuv_ref[...], ckvn)).astype(BF16)
    two = 2 * HEAD_SLAB
    for hp in range(MLA_HEADS // 2):
        kn = _dot(ckvn, wuk_ref[:, hp * two:(hp + 1) * two])
        for hh in range(2):
            sl = slice(hh * HEAD_SLAB, (hh + 1) * HEAD_SLAB)
            k_ref[0, :, hp * two + hh * HEAD_SLAB:hp * two + (hh + 1) * HEAD_SLAB] = \
                (kn[:, sl] + kr).astype(BF16)

    scale = (QK_NOPE + QK_ROPE) ** -0.5 * LOG2_E
    n_groups = MLA_HEADS // 2
    strips_per_group = 2 * MLSTM_W // LANES // n_groups
    for hp in range(n_groups):
        qa = _dot(cqn, wuqa_ref[:, hp * two:(hp + 1) * two])
        qb = _dot(cqn, wuqb_ref[:, hp * two:(hp + 1) * two])
        for hh in range(2):
            sl = slice(hh * HEAD_SLAB, (hh + 1) * HEAD_SLAB)
            q_ref[0, :, hp * two + hh * HEAD_SLAB:hp * two + (hh + 1) * HEAD_SLAB] = \
                ((qa[:, sl] * cos + qb[:, sl] * sin) * scale).astype(BF16)
        for s in range(hp * strips_per_group, (hp + 1) * strips_per_group):
            lanes = slice(s * LANES, (s + 1) * LANES)
            y = cb_ref[:, lanes] + cw_ref[MLSTM_CONV - 1:MLSTM_CONV, lanes] * qk_buf[s, top:top + tm, :]
            for j in range(1, MLSTM_CONV):
                y = y + (cw_ref[MLSTM_CONV - 1 - j:MLSTM_CONV - j, lanes]
                         * qk_buf[s, top - j:top - j + tm, :])
            y = y * jax.nn.sigmoid(y)
            if s * LANES < MLSTM_W:
                qm_ref[0, :, lanes] = y.astype(BF16)
            else:
                km_ref[0, :, s * LANES - MLSTM_W:(s + 1) * LANES - MLSTM_W] = \
                    (y * (MLSTM_HEAD_DIM ** -0.5)).astype(BF16)


def _in_proj(x, mod, nw, w_a, qnw, kvnw, wuqa, wuqb, wuk, wuv, wvm, cw, cb, gb, cosp, sinp, *, tm, tl):
    bsz, s_len, d = x.shape
    row = lambda w: pl.BlockSpec((1, tm, w), lambda b, s: (b, s, 0))
    slabs = MLA_HEADS * HEAD_SLAB
    out_widths = (slabs, slabs, LANES, MLSTM_W, MLSTM_W, LANES, LANES)
    out_shape = [jax.ShapeDtypeStruct((bsz, s_len, w), BF16) for w in out_widths]
    out_specs = [row(w) for w in out_widths]
    out_shape[2] = jax.ShapeDtypeStruct((bsz, s_len // tm, slabs, tm), BF16)
    out_specs[2] = pl.BlockSpec((1, 1, slabs, tm), lambda b, s: (b, s, 0, 0))
    out_shape[5] = jax.ShapeDtypeStruct((bsz, s_len // tl, MLSTM_W, tl), BF16)
    out_specs[5] = pl.BlockSpec((1, tm // tl, MLSTM_W, tl), lambda b, s: (b, s, 0, 0))
    out_shape[6] = jax.ShapeDtypeStruct((bsz, s_len, LANES), F32)
    out_specs[6] = row(LANES)
    return pl.pallas_call(
        functools.partial(_in_proj_kernel, tm=tm, tl=tl),
        grid=(bsz, s_len // tm),
        in_specs=[row(d),
                  pl.BlockSpec((1, 6, d), lambda b, s: (b, 0, 0)),
                  _resident(nw.shape), _resident(w_a.shape), _resident(qnw.shape),
                  _resident(kvnw.shape), _resident(wuqa.shape), _resident(wuqb.shape),
                  _resident(wuk.shape), _resident(wuv.shape), _resident(wvm.shape),
                  _resident(cw.shape),
                  _resident(cb.shape), _resident(gb.shape), row(LANES), row(LANES)],
        out_specs=out_specs,
        out_shape=out_shape,
        scratch_shapes=[pltpu.VMEM((SUBLANES, 2 * MLSTM_W), F32),
                        pltpu.VMEM((2 * MLSTM_W // LANES, SUBLANES + tm, LANES), F32)],
        compiler_params=_params(2),
        name="in_proj",
    )(x, mod, nw, w_a, qnw, kvnw, wuqa, wuqb, wuk, wuv, wvm, cw, cb, gb, cosp, sinp)


def _attn_kernel(q_ref, k_ref, vt_ref, o_ref, *scratch, tq):
    nh = ATTN_HEADS
    qi = pl.program_id(2)
    key_chunk = lax.broadcasted_iota(jnp.int32, (tq, 1), 0) // CHUNK
    qry_chunk = lax.broadcasted_iota(jnp.int32, (1, tq), 1) // CHUNK
    visible = key_chunk <= qry_chunk
    heads = [slice(hh * HEAD_SLAB, (hh + 1) * HEAD_SLAB) for hh in range(nh)]
    s_buf, mx_buf, m_ref, acc_ref = (scratch[i::4] for i in range(4))

    def scores(j, hh, diagonal):
        start = pl.multiple_of(j * tq, tq)
        st = _dot_nt(k_ref[0, pl.ds(start, tq), heads[hh]], q_ref[0, :, heads[hh]])
        if diagonal:
            st = jnp.where(visible, st, NEG)
        s_buf[hh][...] = st
        mx_buf[hh][...] = jnp.max(st, axis=0, keepdims=True)

    def accumulate(j, hh):
        m_old = m_ref[hh][...]
        m_new = jnp.maximum(m_old, mx_buf[hh][...])
        p = jnp.exp2(s_buf[hh][...] - m_new).astype(BF16)
        acc_ref[hh][...] = (jnp.exp2(m_old - m_new) * acc_ref[hh][...]
                            + _dot(vt_ref[0, j, heads[hh], :], p))
        m_ref[hh][...] = m_new

    def block(j, diagonal, next_diagonal):
        for hh in range(nh):
            if hh + 1 < nh:
                scores(j, hh + 1, diagonal)
            elif next_diagonal is not None:
                scores(j + 1, 0, next_diagonal)
            accumulate(j, hh)

    for hh in range(nh):
        m_ref[hh][...] = jnp.full(m_ref[hh].shape, NEG, F32)
        acc_ref[hh][...] = jnp.zeros(acc_ref[hh].shape, F32)

    @pl.when(qi == 0)
    def _():
        scores(0, 0, True)

    @pl.when(qi > 0)
    def _():
        scores(0, 0, False)

        def body(j, carry):
            block(j, False, False)
            return carry

        lax.fori_loop(0, qi - 1, body, 0)
        block(qi - 1, False, True)

    block(qi, True, None)
    outs = [acc_ref[hh][0:V_HEAD, :] / acc_ref[hh][V_HEAD:V_HEAD + 1, :] for hh in range(nh)]
    o_ref[0] = jnp.concatenate(outs, axis=0).T.astype(BF16)


def _attention(q, k, vt, *, tq):
    bsz, s_len, _ = q.shape
    width = ATTN_HEADS * HEAD_SLAB
    n_kv = s_len // tq
    return pl.pallas_call(
        functools.partial(_attn_kernel, tq=tq),
        grid=(bsz, MLA_HEADS // ATTN_HEADS, n_kv),
        in_specs=[pl.BlockSpec((1, tq, width), lambda b, h, i: (b, i, h)),
                  pl.BlockSpec((1, s_len, width), lambda b, h, i: (b, 0, h)),
                  pl.BlockSpec((1, n_kv, width, tq), lambda b, h, i: (b, 0, h, 0))],
        out_specs=pl.BlockSpec((1, tq, ATTN_HEADS * V_HEAD), lambda b, h, i: (b, i, h)),
        out_shape=jax.ShapeDtypeStruct((bsz, s_len, MLA_W), BF16),
        scratch_shapes=[pltpu.VMEM((tq, tq), F32), pltpu.VMEM((1, tq), F32),
                        pltpu.VMEM((1, tq), F32), pltpu.VMEM((HEAD_SLAB, tq), F32)] * ATTN_HEADS,
        compiler_params=_params(3),
        name="mla_attention",
    )(q, k, vt)


def _split3(a):
    hi = a.astype(BF16)
    r = a - hi.astype(F32)
    mid = r.astype(BF16)
    lo = (r - mid.astype(F32)).astype(BF16)
    return hi, mid, lo


def _mlstm_kernel(q_ref, k_ref, vt_ref, g_ref, nw_ref, o_ref, m_st, *state, tl):
    c_st, n_st = state[0::2], state[1::2]

    @pl.when(pl.program_id(1) == 0)
    def _():
        for ref in state + (m_st,):
            ref[...] = jnp.zeros(ref.shape, F32)

    nh, hd_w = MLSTM_HEADS, MLSTM_HEAD_DIM
    g = g_ref[0] * LOG2_E
    key = lax.broadcasted_iota(jnp.int32, (tl, 1), 0)
    qry = lax.broadcasted_iota(jnp.int32, (1, tl), 1)
    visible = key <= qry
    tri = jnp.where(qry <= key, 1.0, 0.0).astype(BF16)
    b_all = sum(_dot(tri, part) for part in _split3(g))
    a_cols = g - pltpu.roll(b_all, LANES - nh, axis=1)
    i_rows = g.T[0:nh]
    b_rows = b_all.T[nh:2 * nh]
    a_rows = i_rows - b_rows
    m_prev = m_st[:, 0:1]
    b_last = b_rows[:, tl - 1:tl]
    log_inter = b_rows + m_prev
    log_w = b_last + a_rows
    m_new = jnp.maximum(b_last + m_prev, jnp.max(log_w, axis=1, keepdims=True))
    w_keys = jnp.exp2(log_w - m_new)
    decay = jnp.exp2(b_last + m_prev - m_new)
    m_st[...] = jnp.broadcast_to(m_new, m_st.shape)
    lane = lax.broadcasted_iota(jnp.int32, (1, PAIR), 1)
    halves = (lane < hd_w, lane >= hd_w)

    h_rows = []
    for p in range(nh // 2):
        sl = slice(p * PAIR, (p + 1) * PAIR)
        qp = q_ref[0, :, sl]
        kp = k_ref[0, :, sl]
        vt = vt_ref[0, 0, sl, :]
        inter_t = _dot_nt(c_st[p][...].astype(BF16), qp)
        qn = _dot_nt(n_st[p][...].astype(BF16), qp)
        for hh in range(2):
            hd = 2 * p + hh
            rows = slice(hh * hd_w, (hh + 1) * hd_w)
            log_d = jnp.where(visible, a_cols[:, hd:hd + 1] + b_rows[hd:hd + 1], NEG)
            m_t = jnp.maximum(log_inter[hd:hd + 1], jnp.max(log_d, axis=0, keepdims=True))
            w_inter = jnp.exp2(log_inter[hd:hd + 1] - m_t)
            q_h = jnp.where(halves[hh], qp, jnp.zeros_like(qp))
            s_t = _dot_nt(kp, q_h) * jnp.exp2(log_d - m_t)
            num = _dot(vt[rows], s_t.astype(BF16)) + w_inter * inter_t[rows]
            den = jnp.sum(s_t, axis=0, keepdims=True) + w_inter * qn[hh:hh + 1]
            h_t = num / jnp.maximum(jnp.abs(den), jnp.exp2(-m_t))
            ms = jnp.mean(h_t * h_t, axis=0, keepdims=True)
            h_rows.append(h_t * lax.rsqrt(ms + EPS))
        vw = jnp.concatenate([vt[hh * hd_w:(hh + 1) * hd_w].astype(F32) * w_keys[2 * p + hh:2 * p + hh + 1]
                              for hh in range(2)], axis=0).astype(BF16)
        upd = _dot(vw, kp)
        nk = _dot(w_keys.astype(BF16), kp)
        for hh in range(2):
            hd = 2 * p + hh
            rows = slice(hh * hd_w, (hh + 1) * hd_w)
            d = decay[hd:hd + 1]
            c_st[p][rows, :] = d * c_st[p][rows, :] + jnp.where(halves[hh], upd[rows], 0.0)
            n_st[p][hh:hh + 1, :] = (d * n_st[p][hh:hh + 1, :]
                                     + jnp.where(halves[hh], nk[hd:hd + 1], 0.0))
    o_ref[0] = jnp.concatenate(h_rows, axis=0).T * nw_ref[...]


def _mlstm(qm, km, vmt, gates, head_nw, *, tl):
    bsz, s_len, w = qm.shape
    row = lambda width: pl.BlockSpec((1, tl, width), lambda b, s: (b, s, 0))
    n_pairs = MLSTM_HEADS // 2
    return pl.pallas_call(
        functools.partial(_mlstm_kernel, tl=tl),
        grid=(bsz, s_len // tl),
        in_specs=[row(w), row(w), pl.BlockSpec((1, 1, w, tl), lambda b, s: (b, s, 0, 0)),
                  row(LANES), _resident(head_nw.shape)],
        out_specs=row(w),
        out_shape=jax.ShapeDtypeStruct((bsz, s_len, w), F32),
        scratch_shapes=[pltpu.VMEM((MLSTM_HEADS, LANES), F32)]
        + [pltpu.VMEM((PAIR, PAIR), F32), pltpu.VMEM((SUBLANES, PAIR), F32)] * n_pairs,
        compiler_params=_params(2),
        name="mlstm",
    )(qm, km, vmt, gates, head_nw)


def _mix_out_kernel(x_ref, mod_ref, nw_ref, wc_ref, ya_ref, hm_ref, wba_ref, wbm_ref, wo_ref,
                    o_ref):
    mod = mod_ref[0]
    x = x_ref[0]
    hb = (_rms(x) * nw_ref[...] * (1.0 + mod[1:2]) + mod[0:1]).astype(BF16)
    o_gate = jax.nn.sigmoid(_dot(hb, wc_ref[:, C_O:C_O + MLSTM_W]))
    y_mlstm = (o_gate * hm_ref[0]).astype(BF16)
    merged = jax.nn.sigmoid(_dot(hb, wc_ref[:, C_GA:C_GA + D_MODEL])) * _dot(ya_ref[0], wba_ref[...])
    merged = merged + (jax.nn.sigmoid(_dot(hb, wc_ref[:, C_GM:C_GM + D_MODEL]))
                       * _dot(y_mlstm, wbm_ref[...]))
    o_ref[0] = x + mod[2:3] * _dot(merged.astype(BF16), wo_ref[...])


def _mix_out(x, mod, nw, w_c, y_mla, hm, wba, wbm, wo, *, tm):
    bsz, s_len, d = x.shape
    row = lambda w: pl.BlockSpec((1, tm, w), lambda b, s: (b, s, 0))
    return pl.pallas_call(
        _mix_out_kernel,
        grid=(bsz, s_len // tm),
        in_specs=[row(d), pl.BlockSpec((1, 6, d), lambda b, s: (b, 0, 0)),
                  _resident(nw.shape), _resident(w_c.shape), row(MLA_W), row(MLSTM_W),
                  _resident(wba.shape), _resident(wbm.shape), _resident(wo.shape)],
        out_specs=row(d),
        out_shape=jax.ShapeDtypeStruct((bsz, s_len, d), F32),
        compiler_params=_params(2),
        name="mix_out",
    )(x, mod, nw, w_c, y_mla, hm, wba, wbm, wo)


def _gelu_tanh(a):
    k1 = -2.0 * 0.7978845608028654 * LOG2_E
    k2 = k1 * 0.044715
    return a / (1.0 + jnp.exp2(a * (k1 + k2 * (a * a))))


def _ffn_kernel(x_ref, mod_ref, nw_ref, wup_ref, cw_ref, cb_ref, wdn_ref, fnw_ref, o_ref,
                halo_ref, acc_ref, *ubuf, tm, final_norm):
    mod = mod_ref[0]
    x = x_ref[0]
    hb = (_rms(x) * nw_ref[...] * (1.0 + mod[4:5]) + mod[3:4]).astype(BF16)

    @pl.when(pl.program_id(1) == 0)
    def _():
        halo_ref[...] = jnp.zeros(halo_ref.shape, F32)

    def cols(c, half):
        lo = half * D_FF + c * FF_CHUNK
        return slice(lo, lo + FF_CHUNK)

    strips = FF_CHUNK // LANES
    top = SUBLANES

    def up(c):
        for half in range(2):
            sl = cols(c, half)
            u = _dot(hb, wup_ref[:, sl])
            buf = ubuf[2 * (c % FF_AHEAD) + half]
            for k in range(strips):
                lanes = slice(k * LANES, (k + 1) * LANES)
                buf[k, 0:top, :] = halo_ref[:, sl][:, lanes]
                buf[k, top:top + tm, :] = u[:, lanes]
            halo_ref[:, sl] = u[tm - SUBLANES:tm, :]

    def conv_strip(c, half, k):
        lo = half * D_FF + c * FF_CHUNK + k * LANES
        sl = slice(lo, lo + LANES)
        buf = ubuf[2 * (c % FF_AHEAD) + half]
        y = cb_ref[:, sl] + cw_ref[FFN_CONV - 1:FFN_CONV, sl] * buf[k, top:top + tm, :]
        for j in range(1, FFN_CONV):
            y = y + cw_ref[FFN_CONV - 1 - j:FFN_CONV - j, sl] * buf[k, top - j:top - j + tm, :]
        return y

    def down(c):
        gated = [_gelu_tanh(conv_strip(c, 0, k)) * conv_strip(c, 1, k) for k in range(strips)]
        gated = jnp.concatenate(gated, axis=1).astype(BF16)
        return _dot(gated, wdn_ref[c * FF_CHUNK:(c + 1) * FF_CHUNK, :])

    for c in range(FF_AHEAD - 1):
        up(c)
    pending = None
    for c in range(N_FF):
        if c + FF_AHEAD - 1 < N_FF:
            up(c + FF_AHEAD - 1)
        part = down(c)
        if c % 2 == 0 and c + 1 < N_FF:
            pending = part
            continue
        if pending is not None:
            part = pending + part
            pending = None
        if c <= 1:
            acc_ref[...] = part
        else:
            acc_ref[...] += part
    y = x + mod[5:6] * acc_ref[...]
    if final_norm:
        y = _rms(y) * fnw_ref[...]
    o_ref[0] = y


def _ffn(x, mod, nw, wup, cw, cb, wdn, fnw, *, tm, final_norm):
    bsz, s_len, d = x.shape
    row = lambda w: pl.BlockSpec((1, tm, w), lambda b, s: (b, s, 0))
    return pl.pallas_call(
        functools.partial(_ffn_kernel, tm=tm, final_norm=final_norm),
        grid=(bsz, s_len // tm),
        in_specs=[row(d), pl.BlockSpec((1, 6, d), lambda b, s: (b, 0, 0)),
                  _resident(nw.shape), _resident(wup.shape), _resident(cw.shape),
                  _resident(cb.shape), _resident(wdn.shape), _resident(fnw.shape)],
        out_specs=row(d),
        out_shape=jax.ShapeDtypeStruct((bsz, s_len, d), F32),
        scratch_shapes=[pltpu.VMEM((SUBLANES, 2 * D_FF), F32), pltpu.VMEM((tm, d), F32)]
        + [pltpu.VMEM((FF_CHUNK // LANES, SUBLANES + tm, LANES), F32)] * (2 * FF_AHEAD),
        compiler_params=_params(2),
        name="conv_ffn",
    )(x, mod, nw, wup, cw, cb, wdn, fnw)


def _rot_cols(w):
    half = w.shape[-1] // 2
    return jnp.concatenate([-w[..., half:], w[..., :half]], axis=-1)


def _pad_cols(w, before, total):
    pad = [(0, 0)] * (w.ndim - 1) + [(before, total - before - w.shape[-1])]
    return jnp.pad(w, pad)


def _layer_weights(w_in, w_uq, w_ukv, gate_b):
    d = w_in.shape[0]
    o = 0
    seg = {}
    for name, width in (("cq", Q_LORA), ("ckv", KV_LORA), ("kr", QK_ROPE), ("qm", MLSTM_W),
                        ("km", MLSTM_W), ("vm", MLSTM_W), ("om", MLSTM_W), ("im", MLSTM_HEADS),
                        ("fm", MLSTM_HEADS), ("ga", D_MODEL), ("gm", D_MODEL)):
        seg[name] = w_in[:, o:o + width]
        o += width
    gates = _pad_cols(jnp.concatenate([seg["im"], seg["fm"]], axis=1), 0, LANES)
    w_a = jnp.concatenate([seg["cq"], seg["ckv"],
                           _pad_cols(seg["kr"], QK_NOPE, LANES),
                           _pad_cols(_rot_cols(seg["kr"]), QK_NOPE, LANES),
                           seg["qm"], seg["km"], gates], axis=1).astype(BF16)
    wvm = seg["vm"].T.astype(BF16)
    w_c = jnp.concatenate([seg["om"], seg["ga"], seg["gm"]], axis=1).astype(BF16)
    uq = w_uq.reshape(Q_LORA, MLA_HEADS, QK_NOPE + QK_ROPE)
    wuqa = _pad_cols(uq, 0, HEAD_SLAB).reshape(Q_LORA, -1).astype(BF16)
    wuqb = _pad_cols(_rot_cols(uq[..., QK_NOPE:]), QK_NOPE, HEAD_SLAB).reshape(Q_LORA, -1).astype(BF16)
    ukv = w_ukv.reshape(KV_LORA, MLA_HEADS, QK_NOPE + V_HEAD)
    wuk = _pad_cols(ukv[..., :QK_NOPE], 0, HEAD_SLAB).reshape(KV_LORA, -1).astype(BF16)
    wuv = _pad_cols(ukv[..., QK_NOPE:], 0, HEAD_SLAB).reshape(KV_LORA, -1).T.astype(BF16)
    gb = _pad_cols(gate_b.reshape(1, -1), 0, LANES)
    assert w_a.shape == (d, A_COLS) and w_c.shape == (d, C_COLS)
    return w_a, w_c, wuqa, wuqb, wuk, wuv, wvm, gb


def kernel(x, c, positions, ada_w, ada_b, norm_mix_w, w_in, q_norm_w, kv_norm_w, w_uq, w_ukv,
           mlstm_conv_w, mlstm_conv_b, mlstm_gate_b, mlstm_head_norm_w, w_br_mla, w_br_mlstm,
           w_out, norm_ffn_w, ffn_w_up, ffn_conv_w, ffn_conv_b, ffn_w_down, final_norm_w):
    bsz, s_len, d = x.shape
    depth = ada_w.shape[0]
    tm = min(512, s_len)
    tq = min(512, s_len)
    tl = min(256, s_len)

    mod_all = _modulation(c, ada_w, ada_b).reshape(depth, bsz, 6, d)
    cos, sin = _rope_tables(positions)
    ones = jnp.ones((bsz, s_len, QK_NOPE), F32)
    zeros = jnp.zeros((bsz, s_len, HEAD_SLAB - QK_NOPE - QK_ROPE), F32)
    cosp = jnp.concatenate([ones, cos, cos, zeros], axis=-1)
    sinp = jnp.concatenate([jnp.zeros_like(ones), sin, sin, zeros], axis=-1)
    row = lambda a: a.reshape(1, -1)

    for l in range(depth):
        mod = mod_all[l]
        w_a, w_c, wuqa, wuqb, wuk, wuv, wvm, gb = _layer_weights(w_in[l], w_uq[l], w_ukv[l],
                                                                mlstm_gate_b[l])
        q, k, v, qm, km, vm, gates = _in_proj(
            x, mod, row(norm_mix_w[l]), w_a, row(q_norm_w[l]), row(kv_norm_w[l]), wuqa, wuqb,
            wuk, wuv, wvm, mlstm_conv_w[l], row(mlstm_conv_b[l]), gb, cosp, sinp, tm=tm, tl=tl)
        y_mla = _attention(q, k, v, tq=tq)
        hm = _mlstm(qm, km, vm, gates, row(mlstm_head_norm_w[l]), tl=tl)
        x = _mix_out(x, mod, row(norm_mix_w[l]), w_c, y_mla, hm, w_br_mla[l].astype(BF16),
                     w_br_mlstm[l].astype(BF16), w_out[l].astype(BF16), tm=tm)
        x = _ffn(x, mod, row(norm_ffn_w[l]), ffn_w_up[l].astype(BF16), ffn_conv_w[l],
                 row(ffn_conv_b[l]), ffn_w_down[l].astype(BF16), row(final_norm_w), tm=tm,
                 final_norm=(l == depth - 1))
    return x
```

```python
import functools

import jax
import jax.numpy as jnp
from jax import lax
from jax.experimental import pallas as pl
from jax.experimental.pallas import tpu as pltpu

F32 = jnp.float32
BF16 = jnp.bfloat16

D_MODEL = 1024
CHUNK = 64
MLA_HEADS = 8
Q_LORA = 384
KV_LORA = 256
QK_NOPE = 64
QK_ROPE = 32
V_HEAD = 64
ROPE_THETA = 10000.0
MLSTM_HEADS = 8
MLSTM_HEAD_DIM = 64
MLSTM_W = MLSTM_HEADS * MLSTM_HEAD_DIM
MLSTM_CONV = 4
D_FF = 2816
FFN_CONV = 3
MLA_W = MLA_HEADS * V_HEAD
EPS = 1e-6

LANES = 128
SUBLANES = 8
HEAD_SLAB = 128
PAIR = 2 * MLSTM_HEAD_DIM
NEG = -1e30
LOG2_E = 1.4426950408889634

A_CQ = 0
A_CKV = A_CQ + Q_LORA
A_KR = A_CKV + KV_LORA
A_KRR = A_KR + LANES
A_QK = A_KRR + LANES
A_G = A_QK + 2 * MLSTM_W
A_COLS = A_G + LANES
C_O = 0
C_GA = C_O + MLSTM_W
C_GM = C_GA + D_MODEL
C_COLS = C_GM + D_MODEL

FF_CHUNK = 256
N_FF = D_FF // FF_CHUNK
FF_AHEAD = 3
ATTN_HEADS = 8

VMEM_LIMIT = 56 * 1024 * 1024


def _params(n_axes):
    return pltpu.CompilerParams(dimension_semantics=("arbitrary",) * n_axes,
                                vmem_limit_bytes=VMEM_LIMIT)


def _resident(shape):
    nd = len(shape)
    return pl.BlockSpec(shape, lambda *_: (0,) * nd, pipeline_mode=pl.Buffered(1))


def _rms(x):
    return x * lax.rsqrt(jnp.mean(x * x, axis=-1, keepdims=True) + EPS)


def _dot(a, b):
    return jnp.dot(a, b, preferred_element_type=F32)


def _dot_nt(a, b):
    return lax.dot_general(a, b, (((1,), (1,)), ((), ())), preferred_element_type=F32)


def _mod_kernel(c_ref, w_ref, b_ref, o_ref):
    c = c_ref[...]
    c_act = (c * jax.nn.sigmoid(c)).astype(BF16)
    o_ref[0] = _dot(c_act, w_ref[0].astype(BF16)) + b_ref[0]


def _modulation(c, ada_w, ada_b):
    depth, d, n = ada_w.shape
    bsz = c.shape[0]
    tn = 1024
    return pl.pallas_call(
        _mod_kernel,
        grid=(depth, n // tn),
        in_specs=[pl.BlockSpec((bsz, d), lambda l, j: (0, 0)),
                  pl.BlockSpec((1, d, tn), lambda l, j: (l, 0, j)),
                  pl.BlockSpec((1, 1, tn), lambda l, j: (l, 0, j))],
        out_specs=pl.BlockSpec((1, bsz, tn), lambda l, j: (l, 0, j)),
        out_shape=jax.ShapeDtypeStruct((depth, bsz, n), F32),
        compiler_params=_params(2),
        name="modulation",
    )(c, ada_w, ada_b.reshape(depth, 1, n))


def _rope_kernel(pos_ref, inv_ref, cos_ref, sin_ref):
    ang = pos_ref[...].astype(F32) * inv_ref[...]
    cos_ref[...] = jnp.cos(ang)
    sin_ref[...] = jnp.sin(ang)


def _rope_tables(positions):
    bsz, s_len = positions.shape
    half = QK_ROPE // 2
    per_row = LANES // half
    rows = bsz * s_len // per_row
    pos = jnp.repeat(positions.reshape(rows, per_row), half, axis=1)
    inv = ROPE_THETA ** (-jnp.arange(half, dtype=F32) / half)
    inv = jnp.tile(inv, per_row).reshape(1, LANES)
    tr = min(rows, 512)
    cos, sin = pl.pallas_call(
        _rope_kernel,
        grid=(rows // tr,),
        in_specs=[pl.BlockSpec((tr, LANES), lambda i: (i, 0)),
                  pl.BlockSpec((1, LANES), lambda i: (0, 0))],
        out_specs=[pl.BlockSpec((tr, LANES), lambda i: (i, 0))] * 2,
        out_shape=[jax.ShapeDtypeStruct((rows, LANES), F32)] * 2,
        compiler_params=_params(1),
        name="rope_tables",
    )(pos, inv)
    return cos.reshape(bsz, s_len, half), sin.reshape(bsz, s_len, half)


def _in_proj_kernel(x_ref, mod_ref, nw_ref, win_ref, qnw_ref, kvnw_ref, wuqa_ref, wuqb_ref,
                    wuk_ref, wuv_ref, wvm_ref, cw_ref, cb_ref, gb_ref, cos_ref, sin_ref,
                    q_ref, k_ref, v_ref, qm_ref, km_ref, vm_ref, g_ref, halo_ref, qk_buf, *, tm, tq, tl):
    mod = mod_ref[0]
    h = _rms(x_ref[0]) * (nw_ref[...] * (1.0 + mod[1:2])) + mod[0:1]
    hb = h.astype(BF16)
    cos = cos_ref[0]
    sin = sin_ref[0]

    def proj(lo, width):
        return _dot(hb, win_ref[:, lo:lo + width])

    @pl.when(pl.program_id(1) == 0)
    def _():
        halo_ref[...] = jnp.zeros(halo_ref.shape, F32)

    cq = proj(A_CQ, Q_LORA)
    ckv = proj(A_CKV, KV_LORA)
    kr = proj(A_KR, LANES) * cos + proj(A_KRR, LANES) * sin
    g = proj(A_G, LANES) + gb_ref[...]
    qk = proj(A_QK, 2 * MLSTM_W)
    top = SUBLANES
    for s in range(2 * MLSTM_W // LANES):
        lanes = slice(s * LANES, (s + 1) * LANES)
        qk_buf[s, 0:top, :] = halo_ref[:, lanes]
        qk_buf[s, top:top + tm, :] = qk[:, lanes]
    halo_ref[...] = qk[tm - SUBLANES:tm, :]
    vmt = _dot_nt(wvm_ref[...], hb).astype(BF16)
    for c in range(tm // tl):
        vm_ref[0, c] = vmt[:, c * tl:(c + 1) * tl]

    cqn = (_rms(cq) * qnw_ref[...]).astype(BF16)
    ckvn = (_rms(ckv) * kvnw_ref[...]).astype(BF16)

    logf = jnp.minimum(g, 0.0) - jnp.log1p(jnp.exp(-jnp.abs(g)))
    lane = lax.broadcasted_iota(jnp.int32, g.shape, 1)
    g_ref[0] = jnp.where(lane < MLSTM_HEADS, g, logf)

    slab_row = lax.broadcasted_iota(jnp.int32, (MLA_HEADS * HEAD_SLAB, 1), 0) % HEAD_SLAB
    vt = jnp.where(slab_row == V_HEAD, 1.0, _dot_nt(wuv_ref[...], ckvn)).astype(BF16)
    for c in range(tm // tq):
        v_ref[0, c] = vt[:, c * tq:(c + 1) * tq]
    two = 2 * HEAD_SLAB
    for hp in range(MLA_HEADS // 2):
        kn = _dot(ckvn, wuk_ref[:, hp * two:(hp + 1) * two])
        for hh in range(2):
            sl = slice(hh * HEAD_SLAB, (hh + 1) * HEAD_SLAB)
            k_ref[0, :, hp * two + hh * HEAD_SLAB:hp * two + (hh + 1) * HEAD_SLAB] = \
                (kn[:, sl] + kr).astype(BF16)

    scale = (QK_NOPE + QK_ROPE) ** -0.5 * LOG2_E
    n_groups = MLA_HEADS // 2
    strips_per_group = 2 * MLSTM_W // LANES // n_groups
    for hp in range(n_groups):
        qa = _dot(cqn, wuqa_ref[:, hp * two:(hp + 1) * two])
        qb = _dot(cqn, wuqb_ref[:, hp * two:(hp + 1) * two])
        for hh in range(2):
            sl = slice(hh * HEAD_SLAB, (hh + 1) * HEAD_SLAB)
            q_ref[0, :, hp * two + hh * HEAD_SLAB:hp * two + (hh + 1) * HEAD_SLAB] = \
                ((qa[:, sl] * cos + qb[:, sl] * sin) * scale).astype(BF16)
        for s in range(hp * strips_per_group, (hp + 1) * strips_per_group):
            lanes = slice(s * LANES, (s + 1) * LANES)
            y = cb_ref[:, lanes] + cw_ref[MLSTM_CONV - 1:MLSTM_CONV, lanes] * qk_buf[s, top:top + tm, :]
            for j in range(1, MLSTM_CONV):
                y = y + (cw_ref[MLSTM_CONV - 1 - j:MLSTM_CONV - j, lanes]
                         * qk_buf[s, top - j:top - j + tm, :])
            y = y * jax.nn.sigmoid(y)
            if s * LANES < MLSTM_W:
                qm_ref[0, :, lanes] = y.astype(BF16)
            else:
                km_ref[0, :, s * LANES - MLSTM_W:(s + 1) * LANES - MLSTM_W] = \
                    (y * (MLSTM_HEAD_DIM ** -0.5)).astype(BF16)


def _in_proj(x, mod, nw, w_a, qnw, kvnw, wuqa, wuqb, wuk, wuv, wvm, cw, cb, gb, cosp, sinp, *, tm, tq, tl):
    bsz, s_len, d = x.shape
    row = lambda w: pl.BlockSpec((1, tm, w), lambda b, s: (b, s, 0))
    slabs = MLA_HEADS * HEAD_SLAB
    out_widths = (slabs, slabs, LANES, MLSTM_W, MLSTM_W, LANES, LANES)
    out_shape = [jax.ShapeDtypeStruct((bsz, s_len, w), BF16) for w in out_widths]
    out_specs = [row(w) for w in out_widths]
    out_shape[2] = jax.ShapeDtypeStruct((bsz, s_len // tq, slabs, tq), BF16)
    out_specs[2] = pl.BlockSpec((1, tm // tq, slabs, tq), lambda b, s: (b, s, 0, 0))
    out_shape[5] = jax.ShapeDtypeStruct((bsz, s_len // tl, MLSTM_W, tl), BF16)
    out_specs[5] = pl.BlockSpec((1, tm // tl, MLSTM_W, tl), lambda b, s: (b, s, 0, 0))
    out_shape[6] = jax.ShapeDtypeStruct((bsz, s_len, LANES), F32)
    out_specs[6] = row(LANES)
    return pl.pallas_call(
        functools.partial(_in_proj_kernel, tm=tm, tq=tq, tl=tl),
        grid=(bsz, s_len // tm),
        in_specs=[row(d),
                  pl.BlockSpec((1, 6, d), lambda b, s: (b, 0, 0)),
                  _resident(nw.shape), _resident(w_a.shape), _resident(qnw.shape),
                  _resident(kvnw.shape), _resident(wuqa.shape), _resident(wuqb.shape),
                  _resident(wuk.shape), _resident(wuv.shape), _resident(wvm.shape),
                  _resident(cw.shape),
                  _resident(cb.shape), _resident(gb.shape), row(LANES), row(LANES)],
        out_specs=out_specs,
        out_shape=out_shape,
        scratch_shapes=[pltpu.VMEM((SUBLANES, 2 * MLSTM_W), F32),
                        pltpu.VMEM((2 * MLSTM_W // LANES, SUBLANES + tm, LANES), F32)],
        compiler_params=_params(2),
        name="in_proj",
    )(x, mod, nw, w_a, qnw, kvnw, wuqa, wuqb, wuk, wuv, wvm, cw, cb, gb, cosp, sinp)


def _attn_kernel(q_ref, k_ref, vt_ref, o_ref, *scratch, tq):
    nh = ATTN_HEADS
    qi = pl.program_id(2)
    half = tq // 2
    heads = [slice(hh * HEAD_SLAB, (hh + 1) * HEAD_SLAB) for hh in range(nh)]
    s_buf, mx_buf, m_ref, acc_ref = (scratch[i::4] for i in range(4))

    def chunk_visible(n_keys, n_queries):
        key_chunk = lax.broadcasted_iota(jnp.int32, (n_keys, 1), 0) // CHUNK
        qry_chunk = lax.broadcasted_iota(jnp.int32, (1, n_queries), 1) // CHUNK
        return key_chunk <= qry_chunk

    def scores(j, hh, diagonal):
        start = pl.multiple_of(j * tq, tq)
        if not diagonal:
            st = _dot_nt(k_ref[0, pl.ds(start, tq), heads[hh]], q_ref[0, :, heads[hh]])
            s_buf[hh][...] = st
            mx_buf[hh][...] = jnp.max(st, axis=0, keepdims=True)
            return
        sa = _dot_nt(k_ref[0, pl.ds(start, half), heads[hh]], q_ref[0, :, heads[hh]])
        sa = jnp.where(chunk_visible(half, tq), sa, NEG)
        sb = _dot_nt(k_ref[0, pl.ds(start + half, half), heads[hh]], q_ref[0, half:tq, heads[hh]])
        sb = jnp.where(chunk_visible(half, half), sb, NEG)
        s_buf[hh][0:half, :] = sa
        s_buf[hh][half:tq, half:tq] = sb
        ma = jnp.max(sa, axis=0, keepdims=True)
        mx_buf[hh][:, 0:half] = ma[:, 0:half]
        mx_buf[hh][:, half:tq] = jnp.maximum(ma[:, half:tq], jnp.max(sb, axis=0, keepdims=True))

    def accumulate(j, hh, diagonal):
        m_old = m_ref[hh][...]
        m_new = jnp.maximum(m_old, mx_buf[hh][...])
        alpha = jnp.exp2(m_old - m_new)
        if not diagonal:
            p = jnp.exp2(s_buf[hh][...] - m_new).astype(BF16)
            acc_ref[hh][...] = alpha * acc_ref[hh][...] + _dot(vt_ref[0, j, heads[hh], :], p)
        else:
            pa = jnp.exp2(s_buf[hh][0:half, :] - m_new).astype(BF16)
            pb = jnp.exp2(s_buf[hh][half:tq, half:tq] - m_new[:, half:tq]).astype(BF16)
            acc_ref[hh][...] = alpha * acc_ref[hh][...] + _dot(vt_ref[0, j, heads[hh], 0:half], pa)
            acc_ref[hh][:, half:tq] += _dot(vt_ref[0, j, heads[hh], half:tq], pb)
        m_ref[hh][...] = m_new

    def block(j, diagonal, next_diagonal):
        for hh in range(nh):
            if hh + 1 < nh:
                scores(j, hh + 1, diagonal)
            elif next_diagonal is not None:
                scores(j + 1, 0, next_diagonal)
            accumulate(j, hh, diagonal)

    for hh in range(nh):
        m_ref[hh][...] = jnp.full(m_ref[hh].shape, NEG, F32)
        acc_ref[hh][...] = jnp.zeros(acc_ref[hh].shape, F32)

    @pl.when(qi == 0)
    def _():
        scores(0, 0, True)

    @pl.when(qi > 0)
    def _():
        scores(0, 0, False)

        def body(j, carry):
            block(j, False, False)
            return carry

        lax.fori_loop(0, qi - 1, body, 0)
        block(qi - 1, False, True)

    block(qi, True, None)
    outs = [acc_ref[hh][0:V_HEAD, :] / acc_ref[hh][V_HEAD:V_HEAD + 1, :] for hh in range(nh)]
    o_ref[0] = jnp.concatenate(outs, axis=0).T.astype(BF16)


def _attention(q, k, vt, *, tq):
    bsz, s_len, _ = q.shape
    width = ATTN_HEADS * HEAD_SLAB
    n_kv = s_len // tq
    return pl.pallas_call(
        functools.partial(_attn_kernel, tq=tq),
        grid=(bsz, MLA_HEADS // ATTN_HEADS, n_kv),
        in_specs=[pl.BlockSpec((1, tq, width), lambda b, h, i: (b, i, h)),
                  pl.BlockSpec((1, s_len, width), lambda b, h, i: (b, 0, h)),
                  pl.BlockSpec((1, n_kv, width, tq), lambda b, h, i: (b, 0, h, 0))],
        out_specs=pl.BlockSpec((1, tq, ATTN_HEADS * V_HEAD), lambda b, h, i: (b, i, h)),
        out_shape=jax.ShapeDtypeStruct((bsz, s_len, MLA_W), BF16),
        scratch_shapes=[pltpu.VMEM((tq, tq), F32), pltpu.VMEM((1, tq), F32),
                        pltpu.VMEM((1, tq), F32), pltpu.VMEM((HEAD_SLAB, tq), F32)] * ATTN_HEADS,
        compiler_params=_params(3),
        name="mla_attention",
    )(q, k, vt)


def _split3(a):
    hi = a.astype(BF16)
    r = a - hi.astype(F32)
    mid = r.astype(BF16)
    lo = (r - mid.astype(F32)).astype(BF16)
    return hi, mid, lo


def _mlstm_kernel(q_ref, k_ref, vt_ref, g_ref, nw_ref, o_ref, m_st, *state, tl):
    c_st, n_st = state[0::2], state[1::2]

    @pl.when(pl.program_id(1) == 0)
    def _():
        for ref in state + (m_st,):
            ref[...] = jnp.zeros(ref.shape, F32)

    nh, hd_w = MLSTM_HEADS, MLSTM_HEAD_DIM
    g = g_ref[0] * LOG2_E
    key = lax.broadcasted_iota(jnp.int32, (tl, 1), 0)
    qry = lax.broadcasted_iota(jnp.int32, (1, tl), 1)
    visible = key <= qry
    tri = jnp.where(qry <= key, 1.0, 0.0).astype(BF16)
    b_all = sum(_dot(tri, part) for part in _split3(g))
    a_cols = g - pltpu.roll(b_all, LANES - nh, axis=1)
    i_rows = g.T[0:nh]
    b_rows = b_all.T[nh:2 * nh]
    a_rows = i_rows - b_rows
    m_prev = m_st[:, 0:1]
    b_last = b_rows[:, tl - 1:tl]
    log_inter = b_rows + m_prev
    log_w = b_last + a_rows
    m_new = jnp.maximum(b_last + m_prev, jnp.max(log_w, axis=1, keepdims=True))
    w_keys = jnp.exp2(log_w - m_new)
    decay = jnp.exp2(b_last + m_prev - m_new)
    m_st[...] = jnp.broadcast_to(m_new, m_st.shape)
    lane = lax.broadcasted_iota(jnp.int32, (1, PAIR), 1)
    halves = (lane < hd_w, lane >= hd_w)

    h_rows = []
    for p in range(nh // 2):
        sl = slice(p * PAIR, (p + 1) * PAIR)
        qp = q_ref[0, :, sl]
        kp = k_ref[0, :, sl]
        vt = vt_ref[0, 0, sl, :]
        inter_t = _dot_nt(c_st[p][...].astype(BF16), qp)
        qn = _dot_nt(n_st[p][...].astype(BF16), qp)
        for hh in range(2):
            hd = 2 * p + hh
            rows = slice(hh * hd_w, (hh + 1) * hd_w)
            log_d = jnp.where(visible, a_cols[:, hd:hd + 1] + b_rows[hd:hd + 1], NEG)
            m_t = jnp.maximum(log_inter[hd:hd + 1], jnp.max(log_d, axis=0, keepdims=True))
            w_inter = jnp.exp2(log_inter[hd:hd + 1] - m_t)
            q_h = jnp.where(halves[hh], qp, jnp.zeros_like(qp))
            s_t = _dot_nt(kp, q_h) * jnp.exp2(log_d - m_t)
            num = _dot(vt[rows], s_t.astype(BF16)) + w_inter * inter_t[rows]
            den = jnp.sum(s_t, axis=0, keepdims=True) + w_inter * qn[hh:hh + 1]
            h_t = num / jnp.maximum(jnp.abs(den), jnp.exp2(-m_t))
            ms = jnp.mean(h_t * h_t, axis=0, keepdims=True)
            h_rows.append(h_t * lax.rsqrt(ms + EPS))
        vw = jnp.concatenate([vt[hh * hd_w:(hh + 1) * hd_w].astype(F32) * w_keys[2 * p + hh:2 * p + hh + 1]
                              for hh in range(2)], axis=0).astype(BF16)
        upd = _dot(vw, kp)
        nk = _dot(w_keys.astype(BF16), kp)
        for hh in range(2):
            hd = 2 * p + hh
            rows = slice(hh * hd_w, (hh + 1) * hd_w)
            d = decay[hd:hd + 1]
            c_st[p][rows, :] = d * c_st[p][rows, :] + jnp.where(halves[hh], upd[rows], 0.0)
            n_st[p][hh:hh + 1, :] = (d * n_st[p][hh:hh + 1, :]
                                     + jnp.where(halves[hh], nk[hd:hd + 1], 0.0))
    o_ref[0] = jnp.concatenate(h_rows, axis=0).T * nw_ref[...]


def _mlstm(qm, km, vmt, gates, head_nw, *, tl):
    bsz, s_len, w = qm.shape
    row = lambda width: pl.BlockSpec((1, tl, width), lambda b, s: (b, s, 0))
    n_pairs = MLSTM_HEADS // 2
    return pl.pallas_call(
        functools.partial(_mlstm_kernel, tl=tl),
        grid=(bsz, s_len // tl),
        in_specs=[row(w), row(w), pl.BlockSpec((1, 1, w, tl), lambda b, s: (b, s, 0, 0)),
                  row(LANES), _resident(head_nw.shape)],
        out_specs=row(w),
        out_shape=jax.ShapeDtypeStruct((bsz, s_len, w), F32),
        scratch_shapes=[pltpu.VMEM((MLSTM_HEADS, LANES), F32)]
        + [pltpu.VMEM((PAIR, PAIR), F32), pltpu.VMEM((SUBLANES, PAIR), F32)] * n_pairs,
        compiler_params=_params(2),
        name="mlstm",
    )(qm, km, vmt, gates, head_nw)


def _mix_out_kernel(x_ref, mod_ref, nw_ref, wc_ref, ya_ref, hm_ref, wba_ref, wbm_ref, wo_ref,
                    o_ref):
    mod = mod_ref[0]
    x = x_ref[0]
    hb = (_rms(x) * (nw_ref[...] * (1.0 + mod[1:2])) + mod[0:1]).astype(BF16)
    o_gate = jax.nn.sigmoid(_dot(hb, wc_ref[:, C_O:C_O + MLSTM_W]))
    y_mlstm = (o_gate * hm_ref[0]).astype(BF16)
    merged = jax.nn.sigmoid(_dot(hb, wc_ref[:, C_GA:C_GA + D_MODEL])) * _dot(ya_ref[0], wba_ref[...])
    merged = merged + (jax.nn.sigmoid(_dot(hb, wc_ref[:, C_GM:C_GM + D_MODEL]))
                       * _dot(y_mlstm, wbm_ref[...]))
    o_ref[0] = x + mod[2:3] * _dot(merged.astype(BF16), wo_ref[...])


def _mix_out(x, mod, nw, w_c, y_mla, hm, wba, wbm, wo, *, tm):
    bsz, s_len, d = x.shape
    row = lambda w: pl.BlockSpec((1, tm, w), lambda b, s: (b, s, 0))
    return pl.pallas_call(
        _mix_out_kernel,
        grid=(bsz, s_len // tm),
        in_specs=[row(d), pl.BlockSpec((1, 6, d), lambda b, s: (b, 0, 0)),
                  _resident(nw.shape), _resident(w_c.shape), row(MLA_W), row(MLSTM_W),
                  _resident(wba.shape), _resident(wbm.shape), _resident(wo.shape)],
        out_specs=row(d),
        out_shape=jax.ShapeDtypeStruct((bsz, s_len, d), F32),
        compiler_params=_params(2),
        name="mix_out",
    )(x, mod, nw, w_c, y_mla, hm, wba, wbm, wo)


def _gelu_tanh(a):
    k1 = -2.0 * 0.7978845608028654 * LOG2_E
    k2 = k1 * 0.044715
    return a / (1.0 + jnp.exp2(a * (k1 + k2 * (a * a))))


def _ffn_kernel(x_ref, mod_ref, nw_ref, wup_ref, cw_ref, cb_ref, wdn_ref, fnw_ref, o_ref,
                halo_ref, acc_ref, *ubuf, tm, final_norm):
    mod = mod_ref[0]
    x = x_ref[0]
    hb = (_rms(x) * (nw_ref[...] * (1.0 + mod[4:5])) + mod[3:4]).astype(BF16)

    @pl.when(pl.program_id(1) == 0)
    def _():
        halo_ref[...] = jnp.zeros(halo_ref.shape, F32)

    def cols(c, half):
        lo = half * D_FF + c * FF_CHUNK
        return slice(lo, lo + FF_CHUNK)

    strips = FF_CHUNK // LANES
    top = SUBLANES

    def up(c):
        for half in range(2):
            sl = cols(c, half)
            u = _dot(hb, wup_ref[:, sl])
            buf = ubuf[2 * (c % FF_AHEAD) + half]
            for k in range(strips):
                lanes = slice(k * LANES, (k + 1) * LANES)
                buf[k, 0:top, :] = halo_ref[:, sl][:, lanes]
                buf[k, top:top + tm, :] = u[:, lanes]
            halo_ref[:, sl] = u[tm - SUBLANES:tm, :]

    def conv_strip(c, half, k):
        lo = half * D_FF + c * FF_CHUNK + k * LANES
        sl = slice(lo, lo + LANES)
        buf = ubuf[2 * (c % FF_AHEAD) + half]
        y = cb_ref[:, sl] + cw_ref[FFN_CONV - 1:FFN_CONV, sl] * buf[k, top:top + tm, :]
        for j in range(1, FFN_CONV):
            y = y + cw_ref[FFN_CONV - 1 - j:FFN_CONV - j, sl] * buf[k, top - j:top - j + tm, :]
        return y

    def down(c):
        gated = [_gelu_tanh(conv_strip(c, 0, k)) * conv_strip(c, 1, k) for k in range(strips)]
        gated = jnp.concatenate(gated, axis=1).astype(BF16)
        return _dot(gated, wdn_ref[c * FF_CHUNK:(c + 1) * FF_CHUNK, :])

    for c in range(FF_AHEAD - 1):
        up(c)
    pending = None
    for c in range(N_FF):
        if c + FF_AHEAD - 1 < N_FF:
            up(c + FF_AHEAD - 1)
        part = down(c)
        if c % 2 == 0 and c + 1 < N_FF:
            pending = part
            continue
        if pending is not None:
            part = pending + part
            pending = None
        if c <= 1:
            acc_ref[...] = part
        else:
            acc_ref[...] += part
    y = x + mod[5:6] * acc_ref[...]
    if final_norm:
        y = _rms(y) * fnw_ref[...]
    o_ref[0] = y


def _ffn(x, mod, nw, wup, cw, cb, wdn, fnw, *, tm, final_norm):
    bsz, s_len, d = x.shape
    row = lambda w: pl.BlockSpec((1, tm, w), lambda b, s: (b, s, 0))
    return pl.pallas_call(
        functools.partial(_ffn_kernel, tm=tm, final_norm=final_norm),
        grid=(bsz, s_len // tm),
        in_specs=[row(d), pl.BlockSpec((1, 6, d), lambda b, s: (b, 0, 0)),
                  _resident(nw.shape), _resident(wup.shape), _resident(cw.shape),
                  _resident(cb.shape), _resident(wdn.shape), _resident(fnw.shape)],
        out_specs=row(d),
        out_shape=jax.ShapeDtypeStruct((bsz, s_len, d), F32),
        scratch_shapes=[pltpu.VMEM((SUBLANES, 2 * D_FF), F32), pltpu.VMEM((tm, d), F32)]
        + [pltpu.VMEM((FF_CHUNK // LANES, SUBLANES + tm, LANES), F32)] * (2 * FF_AHEAD),
        compiler_params=_params(2),
        name="conv_ffn",
    )(x, mod, nw, wup, cw, cb, wdn, fnw)


def _rot_cols(w):
    half = w.shape[-1] // 2
    return jnp.concatenate([-w[..., half:], w[..., :half]], axis=-1)


def _pad_cols(w, before, total):
    pad = [(0, 0)] * (w.ndim - 1) + [(before, total - before - w.shape[-1])]
    return jnp.pad(w, pad)


def _layer_weights(w_in, w_uq, w_ukv, gate_b):
    d = w_in.shape[0]
    o = 0
    seg = {}
    for name, width in (("cq", Q_LORA), ("ckv", KV_LORA), ("kr", QK_ROPE), ("qm", MLSTM_W),
                        ("km", MLSTM_W), ("vm", MLSTM_W), ("om", MLSTM_W), ("im", MLSTM_HEADS),
                        ("fm", MLSTM_HEADS), ("ga", D_MODEL), ("gm", D_MODEL)):
        seg[name] = w_in[:, o:o + width]
        o += width
    gates = _pad_cols(jnp.concatenate([seg["im"], seg["fm"]], axis=1), 0, LANES)
    w_a = jnp.concatenate([seg["cq"], seg["ckv"],
                           _pad_cols(seg["kr"], QK_NOPE, LANES),
                           _pad_cols(_rot_cols(seg["kr"]), QK_NOPE, LANES),
                           seg["qm"], seg["km"], gates], axis=1).astype(BF16)
    wvm = seg["vm"].T.astype(BF16)
    w_c = jnp.concatenate([seg["om"], seg["ga"], seg["gm"]], axis=1).astype(BF16)
    uq = w_uq.reshape(Q_LORA, MLA_HEADS, QK_NOPE + QK_ROPE)
    wuqa = _pad_cols(uq, 0, HEAD_SLAB).reshape(Q_LORA, -1).astype(BF16)
    wuqb = _pad_cols(_rot_cols(uq[..., QK_NOPE:]), QK_NOPE, HEAD_SLAB).reshape(Q_LORA, -1).astype(BF16)
    ukv = w_ukv.reshape(KV_LORA, MLA_HEADS, QK_NOPE + V_HEAD)
    wuk = _pad_cols(ukv[..., :QK_NOPE], 0, HEAD_SLAB).reshape(KV_LORA, -1).astype(BF16)
    wuv = _pad_cols(ukv[..., QK_NOPE:], 0, HEAD_SLAB).reshape(KV_LORA, -1).T.astype(BF16)
    gb = _pad_cols(gate_b.reshape(1, -1), 0, LANES)
    assert w_a.shape == (d, A_COLS) and w_c.shape == (d, C_COLS)
    return w_a, w_c, wuqa, wuqb, wuk, wuv, wvm, gb


def kernel(x, c, positions, ada_w, ada_b, norm_mix_w, w_in, q_norm_w, kv_norm_w, w_uq, w_ukv,
           mlstm_conv_w, mlstm_conv_b, mlstm_gate_b, mlstm_head_norm_w, w_br_mla, w_br_mlstm,
           w_out, norm_ffn_w, ffn_w_up, ffn_conv_w, ffn_conv_b, ffn_w_down, final_norm_w):
    bsz, s_len, d = x.shape
    depth = ada_w.shape[0]
    tm = min(1024, s_len)
    tf = min(512, s_len)
    tq = min(512, s_len)
    tl = min(256, s_len)
    assert all(s_len % t == 0 for t in (tm, tf, tq, tl)) and tm % tq == 0 and tm % tl == 0, s_len

    mod_all = _modulation(c, ada_w, ada_b).reshape(depth, bsz, 6, d)
    cos, sin = _rope_tables(positions)
    ones = jnp.ones((bsz, s_len, QK_NOPE), F32)
    zeros = jnp.zeros((bsz, s_len, HEAD_SLAB - QK_NOPE - QK_ROPE), F32)
    cosp = jnp.concatenate([ones, cos, cos, zeros], axis=-1)
    sinp = jnp.concatenate([jnp.zeros_like(ones), sin, sin, zeros], axis=-1)
    row = lambda a: a.reshape(1, -1)

    for l in range(depth):
        mod = mod_all[l]
        w_a, w_c, wuqa, wuqb, wuk, wuv, wvm, gb = _layer_weights(w_in[l], w_uq[l], w_ukv[l],
                                                                mlstm_gate_b[l])
        q, k, v, qm, km, vm, gates = _in_proj(
            x, mod, row(norm_mix_w[l]), w_a, row(q_norm_w[l]), row(kv_norm_w[l]), wuqa, wuqb,
            wuk, wuv, wvm, mlstm_conv_w[l], row(mlstm_conv_b[l]), gb, cosp, sinp, tm=tm, tq=tq, tl=tl)
        y_mla = _attention(q, k, v, tq=tq)
        hm = _mlstm(qm, km, vm, gates, row(mlstm_head_norm_w[l]), tl=tl)
        x = _mix_out(x, mod, row(norm_mix_w[l]), w_c, y_mla, hm, w_br_mla[l].astype(BF16),
                     w_br_mlstm[l].astype(BF16), w_out[l].astype(BF16), tm=tm)
        x = _ffn(x, mod, row(norm_ffn_w[l]), ffn_w_up[l].astype(BF16), ffn_conv_w[l],
                 row(ffn_conv_b[l]), ffn_w_down[l].astype(BF16), row(final_norm_w), tm=tf,
                 final_norm=(l == depth - 1))
    return x
```

```python
import functools

import jax
import jax.numpy as jnp
from jax import lax
from jax.experimental import pallas as pl
from jax.experimental.pallas import tpu as pltpu

F32 = jnp.float32
BF16 = jnp.bfloat16

D_MODEL = 1024
CHUNK = 64
MLA_HEADS = 8
Q_LORA = 384
KV_LORA = 256
QK_NOPE = 64
QK_ROPE = 32
V_HEAD = 64
ROPE_THETA = 10000.0
MLSTM_HEADS = 8
MLSTM_HEAD_DIM = 64
MLSTM_W = MLSTM_HEADS * MLSTM_HEAD_DIM
MLSTM_CONV = 4
D_FF = 2816
FFN_CONV = 3
MLA_W = MLA_HEADS * V_HEAD
EPS = 1e-6

LANES = 128
SUBLANES = 8
HEAD_SLAB = 128
PAIR = 2 * MLSTM_HEAD_DIM
NEG = -1e30
LOG2_E = 1.4426950408889634

A_CQ = 0
A_CKV = A_CQ + Q_LORA
A_KR = A_CKV + KV_LORA
A_KRR = A_KR + LANES
A_QK = A_KRR + LANES
A_G = A_QK + 2 * MLSTM_W
A_COLS = A_G + LANES
C_O = 0
C_GA = C_O + MLSTM_W
C_GM = C_GA + D_MODEL
C_COLS = C_GM + D_MODEL

FF_CHUNK = 256
N_FF = D_FF // FF_CHUNK
FF_AHEAD = 3
ATTN_HEADS = 8

VMEM_LIMIT = 56 * 1024 * 1024


def _params(n_axes):
    return pltpu.CompilerParams(dimension_semantics=("arbitrary",) * n_axes,
                                vmem_limit_bytes=VMEM_LIMIT)


def _resident(shape):
    nd = len(shape)
    return pl.BlockSpec(shape, lambda *_: (0,) * nd, pipeline_mode=pl.Buffered(1))


def _rms(x):
    return x * lax.rsqrt(jnp.mean(x * x, axis=-1, keepdims=True) + EPS)


def _dot(a, b):
    return jnp.dot(a, b, preferred_element_type=F32)


def _dot_nt(a, b):
    return lax.dot_general(a, b, (((1,), (1,)), ((), ())), preferred_element_type=F32)


def _mod_kernel(c_ref, w_ref, b_ref, o_ref):
    c = c_ref[...]
    c_act = (c * jax.nn.sigmoid(c)).astype(BF16)
    o_ref[0] = _dot(c_act, w_ref[0].astype(BF16)) + b_ref[0]


def _modulation(c, ada_w, ada_b):
    depth, d, n = ada_w.shape
    bsz = c.shape[0]
    tn = 1024
    return pl.pallas_call(
        _mod_kernel,
        grid=(depth, n // tn),
        in_specs=[pl.BlockSpec((bsz, d), lambda l, j: (0, 0)),
                  pl.BlockSpec((1, d, tn), lambda l, j: (l, 0, j)),
                  pl.BlockSpec((1, 1, tn), lambda l, j: (l, 0, j))],
        out_specs=pl.BlockSpec((1, bsz, tn), lambda l, j: (l, 0, j)),
        out_shape=jax.ShapeDtypeStruct((depth, bsz, n), F32),
        compiler_params=_params(2),
        name="modulation",
    )(c, ada_w, ada_b.reshape(depth, 1, n))


def _rope_kernel(pos_ref, inv_ref, cos_ref, sin_ref):
    ang = pos_ref[...].astype(F32) * inv_ref[...]
    cos_ref[...] = jnp.cos(ang)
    sin_ref[...] = jnp.sin(ang)


def _rope_tables(positions):
    bsz, s_len = positions.shape
    half = QK_ROPE // 2
    per_row = LANES // half
    rows = bsz * s_len // per_row
    pos = jnp.repeat(positions.reshape(rows, per_row), half, axis=1)
    inv = ROPE_THETA ** (-jnp.arange(half, dtype=F32) / half)
    inv = jnp.tile(inv, per_row).reshape(1, LANES)
    tr = min(rows, 512)
    cos, sin = pl.pallas_call(
        _rope_kernel,
        grid=(rows // tr,),
        in_specs=[pl.BlockSpec((tr, LANES), lambda i: (i, 0)),
                  pl.BlockSpec((1, LANES), lambda i: (0, 0))],
        out_specs=[pl.BlockSpec((tr, LANES), lambda i: (i, 0))] * 2,
        out_shape=[jax.ShapeDtypeStruct((rows, LANES), F32)] * 2,
        compiler_params=_params(1),
        name="rope_tables",
    )(pos, inv)
    return cos.reshape(bsz, s_len, half), sin.reshape(bsz, s_len, half)


def _in_proj_kernel(x_ref, mod_ref, nw_ref, win_ref, qnw_ref, kvnw_ref, wuqa_ref, wuqb_ref,
                    wuk_ref, wuv_ref, wvm_ref, cw_ref, cb_ref, gb_ref, cos_ref, sin_ref,
                    q_ref, k_ref, v_ref, qm_ref, km_ref, vm_ref, g_ref, halo_ref, qk_buf, *, tm, tq, tl):
    mod = mod_ref[0]
    h = _rms(x_ref[0]) * (nw_ref[...] * (1.0 + mod[1:2])) + mod[0:1]
    hb = h.astype(BF16)
    cos = cos_ref[0]
    sin = sin_ref[0]

    def proj(lo, width):
        return _dot(hb, win_ref[:, lo:lo + width])

    @pl.when(pl.program_id(1) == 0)
    def _():
        halo_ref[...] = jnp.zeros(halo_ref.shape, F32)

    cq = proj(A_CQ, Q_LORA)
    ckv = proj(A_CKV, KV_LORA)
    kr = proj(A_KR, LANES) * cos + proj(A_KRR, LANES) * sin
    g = proj(A_G, LANES) + gb_ref[...]
    qk = proj(A_QK, 2 * MLSTM_W)
    top = SUBLANES
    for s in range(2 * MLSTM_W // LANES):
        lanes = slice(s * LANES, (s + 1) * LANES)
        qk_buf[s, 0:top, :] = halo_ref[:, lanes]
        qk_buf[s, top:top + tm, :] = qk[:, lanes]
    halo_ref[...] = qk[tm - SUBLANES:tm, :]
    vmt = _dot_nt(wvm_ref[...], hb).astype(BF16)
    for c in range(tm // tl):
        vm_ref[0, c] = vmt[:, c * tl:(c + 1) * tl]

    cqn = (_rms(cq) * qnw_ref[...]).astype(BF16)
    ckvn = (_rms(ckv) * kvnw_ref[...]).astype(BF16)

    logf = jnp.minimum(g, 0.0) - jnp.log1p(jnp.exp(-jnp.abs(g)))
    lane = lax.broadcasted_iota(jnp.int32, g.shape, 1)
    g_ref[0] = jnp.where(lane < MLSTM_HEADS, g, logf)

    slab_row = lax.broadcasted_iota(jnp.int32, (MLA_HEADS * HEAD_SLAB, 1), 0) % HEAD_SLAB
    vt = jnp.where(slab_row == V_HEAD, 1.0, _dot_nt(wuv_ref[...], ckvn)).astype(BF16)
    for c in range(tm // tq):
        v_ref[0, c] = vt[:, c * tq:(c + 1) * tq]
    two = 2 * HEAD_SLAB
    for hp in range(MLA_HEADS // 2):
        kn = _dot(ckvn, wuk_ref[:, hp * two:(hp + 1) * two])
        for hh in range(2):
            sl = slice(hh * HEAD_SLAB, (hh + 1) * HEAD_SLAB)
            k_ref[0, :, hp * two + hh * HEAD_SLAB:hp * two + (hh + 1) * HEAD_SLAB] = \
                (kn[:, sl] + kr).astype(BF16)

    scale = (QK_NOPE + QK_ROPE) ** -0.5 * LOG2_E
    n_groups = MLA_HEADS // 2
    strips_per_group = 2 * MLSTM_W // LANES // n_groups
    for hp in range(n_groups):
        qa = _dot(cqn, wuqa_ref[:, hp * two:(hp + 1) * two])
        qb = _dot(cqn, wuqb_ref[:, hp * two:(hp + 1) * two])
        for hh in range(2):
            sl = slice(hh * HEAD_SLAB, (hh + 1) * HEAD_SLAB)
            q_ref[0, :, hp * two + hh * HEAD_SLAB:hp * two + (hh + 1) * HEAD_SLAB] = \
                ((qa[:, sl] * cos + qb[:, sl] * sin) * scale).astype(BF16)
        for s in range(hp * strips_per_group, (hp + 1) * strips_per_group):
            lanes = slice(s * LANES, (s + 1) * LANES)
            y = cb_ref[:, lanes] + cw_ref[MLSTM_CONV - 1:MLSTM_CONV, lanes] * qk_buf[s, top:top + tm, :]
            for j in range(1, MLSTM_CONV):
                y = y + (cw_ref[MLSTM_CONV - 1 - j:MLSTM_CONV - j, lanes]
                         * qk_buf[s, top - j:top - j + tm, :])
            y = y * jax.nn.sigmoid(y)
            if s * LANES < MLSTM_W:
                qm_ref[0, :, lanes] = y.astype(BF16)
            else:
                km_ref[0, :, s * LANES - MLSTM_W:(s + 1) * LANES - MLSTM_W] = \
                    (y * (MLSTM_HEAD_DIM ** -0.5)).astype(BF16)


def _in_proj(x, mod, nw, w_a, qnw, kvnw, wuqa, wuqb, wuk, wuv, wvm, cw, cb, gb, cosp, sinp, *, tm, tq, tl):
    bsz, s_len, d = x.shape
    row = lambda w: pl.BlockSpec((1, tm, w), lambda b, s: (b, s, 0))
    slabs = MLA_HEADS * HEAD_SLAB
    out_widths = (slabs, slabs, LANES, MLSTM_W, MLSTM_W, LANES, LANES)
    out_shape = [jax.ShapeDtypeStruct((bsz, s_len, w), BF16) for w in out_widths]
    out_specs = [row(w) for w in out_widths]
    out_shape[2] = jax.ShapeDtypeStruct((bsz, s_len // tq, slabs, tq), BF16)
    out_specs[2] = pl.BlockSpec((1, tm // tq, slabs, tq), lambda b, s: (b, s, 0, 0))
    out_shape[5] = jax.ShapeDtypeStruct((bsz, s_len // tl, MLSTM_W, tl), BF16)
    out_specs[5] = pl.BlockSpec((1, tm // tl, MLSTM_W, tl), lambda b, s: (b, s, 0, 0))
    out_shape[6] = jax.ShapeDtypeStruct((bsz, s_len, LANES), F32)
    out_specs[6] = row(LANES)
    return pl.pallas_call(
        functools.partial(_in_proj_kernel, tm=tm, tq=tq, tl=tl),
        grid=(bsz, s_len // tm),
        in_specs=[row(d),
                  pl.BlockSpec((1, 6, d), lambda b, s: (b, 0, 0)),
                  _resident(nw.shape), _resident(w_a.shape), _resident(qnw.shape),
                  _resident(kvnw.shape), _resident(wuqa.shape), _resident(wuqb.shape),
                  _resident(wuk.shape), _resident(wuv.shape), _resident(wvm.shape),
                  _resident(cw.shape),
                  _resident(cb.shape), _resident(gb.shape), row(LANES), row(LANES)],
        out_specs=out_specs,
        out_shape=out_shape,
        scratch_shapes=[pltpu.VMEM((SUBLANES, 2 * MLSTM_W), F32),
                        pltpu.VMEM((2 * MLSTM_W // LANES, SUBLANES + tm, LANES), F32)],
        compiler_params=_params(2),
        name="in_proj",
    )(x, mod, nw, w_a, qnw, kvnw, wuqa, wuqb, wuk, wuv, wvm, cw, cb, gb, cosp, sinp)


def _attn_kernel(q_ref, k_ref, vt_ref, o_ref, *scratch, tq):
    nh = ATTN_HEADS
    qi = pl.program_id(2)
    half = tq // 2
    heads = [slice(hh * HEAD_SLAB, (hh + 1) * HEAD_SLAB) for hh in range(nh)]
    s_buf, mx_buf, m_ref, acc_ref = (scratch[i::4] for i in range(4))

    def chunk_visible(n_keys, n_queries):
        key_chunk = lax.broadcasted_iota(jnp.int32, (n_keys, 1), 0) // CHUNK
        qry_chunk = lax.broadcasted_iota(jnp.int32, (1, n_queries), 1) // CHUNK
        return key_chunk <= qry_chunk

    def scores(j, hh, diagonal):
        start = pl.multiple_of(j * tq, tq)
        if not diagonal:
            st = _dot_nt(k_ref[0, pl.ds(start, tq), heads[hh]], q_ref[0, :, heads[hh]])
            s_buf[hh][...] = st
            mx_buf[hh][...] = jnp.max(st, axis=0, keepdims=True)
            return
        sa = _dot_nt(k_ref[0, pl.ds(start, half), heads[hh]], q_ref[0, :, heads[hh]])
        sa = jnp.where(chunk_visible(half, tq), sa, NEG)
        sb = _dot_nt(k_ref[0, pl.ds(start + half, half), heads[hh]], q_ref[0, half:tq, heads[hh]])
        sb = jnp.where(chunk_visible(half, half), sb, NEG)
        s_buf[hh][0:half, :] = sa
        s_buf[hh][half:tq, half:tq] = sb
        ma = jnp.max(sa, axis=0, keepdims=True)
        mx_buf[hh][:, 0:half] = ma[:, 0:half]
        mx_buf[hh][:, half:tq] = jnp.maximum(ma[:, half:tq], jnp.max(sb, axis=0, keepdims=True))

    def accumulate(j, hh, diagonal):
        m_old = m_ref[hh][...]
        m_new = jnp.maximum(m_old, mx_buf[hh][...])
        alpha = jnp.exp2(m_old - m_new)
        if not diagonal:
            p = jnp.exp2(s_buf[hh][...] - m_new).astype(BF16)
            acc_ref[hh][...] = alpha * acc_ref[hh][...] + _dot(vt_ref[0, j, heads[hh], :], p)
        else:
            pa = jnp.exp2(s_buf[hh][0:half, :] - m_new).astype(BF16)
            pb = jnp.exp2(s_buf[hh][half:tq, half:tq] - m_new[:, half:tq]).astype(BF16)
            acc_ref[hh][...] = alpha * acc_ref[hh][...] + _dot(vt_ref[0, j, heads[hh], 0:half], pa)
            acc_ref[hh][:, half:tq] += _dot(vt_ref[0, j, heads[hh], half:tq], pb)
        m_ref[hh][...] = m_new

    def block(j, diagonal, next_diagonal):
        for hh in range(nh):
            if hh + 1 < nh:
                scores(j, hh + 1, diagonal)
            elif next_diagonal is not None:
                scores(j + 1, 0, next_diagonal)
            accumulate(j, hh, diagonal)

    for hh in range(nh):
        m_ref[hh][...] = jnp.full(m_ref[hh].shape, NEG, F32)
        acc_ref[hh][...] = jnp.zeros(acc_ref[hh].shape, F32)

    @pl.when(qi == 0)
    def _():
        scores(0, 0, True)

    @pl.when(qi > 0)
    def _():
        scores(0, 0, False)

        def body(j, carry):
            block(j, False, False)
            return carry

        lax.fori_loop(0, qi - 1, body, 0)
        block(qi - 1, False, True)

    block(qi, True, None)
    outs = [acc_ref[hh][0:V_HEAD, :] / acc_ref[hh][V_HEAD:V_HEAD + 1, :] for hh in range(nh)]
    o_ref[0] = jnp.concatenate(outs, axis=0).T.astype(BF16)


def _attention(q, k, vt, *, tq):
    bsz, s_len, _ = q.shape
    width = ATTN_HEADS * HEAD_SLAB
    n_kv = s_len // tq
    return pl.pallas_call(
        functools.partial(_attn_kernel, tq=tq),
        grid=(bsz, MLA_HEADS // ATTN_HEADS, n_kv),
        in_specs=[pl.BlockSpec((1, tq, width), lambda b, h, i: (b, i, h)),
                  pl.BlockSpec((1, s_len, width), lambda b, h, i: (b, 0, h)),
                  pl.BlockSpec((1, n_kv, width, tq), lambda b, h, i: (b, 0, h, 0))],
        out_specs=pl.BlockSpec((1, tq, ATTN_HEADS * V_HEAD), lambda b, h, i: (b, i, h)),
        out_shape=jax.ShapeDtypeStruct((bsz, s_len, MLA_W), BF16),
        scratch_shapes=[pltpu.VMEM((tq, tq), F32), pltpu.VMEM((1, tq), F32),
                        pltpu.VMEM((1, tq), F32), pltpu.VMEM((HEAD_SLAB, tq), F32)] * ATTN_HEADS,
        compiler_params=_params(3),
        name="mla_attention",
    )(q, k, vt)


def _split3(a):
    hi = a.astype(BF16)
    r = a - hi.astype(F32)
    mid = r.astype(BF16)
    lo = (r - mid.astype(F32)).astype(BF16)
    return hi, mid, lo


def _mlstm_kernel(q_ref, k_ref, vt_ref, g_ref, nw_ref, o_ref, m_st, *state, tl, nc):
    @pl.when(pl.program_id(1) == 0)
    def _():
        for ref in state + (m_st,):
            ref[...] = jnp.zeros(ref.shape, F32)

    nh, hd_w = MLSTM_HEADS, MLSTM_HEAD_DIM
    pad = 2 * SUBLANES
    key = lax.broadcasted_iota(jnp.int32, (tl, 1), 0)
    qry = lax.broadcasted_iota(jnp.int32, (1, tl), 1)
    visible = key <= qry
    tri = jnp.where(qry <= key, 1.0, 0.0).astype(BF16)
    ones_rows = jnp.ones((pad, tl), BF16)
    lane = lax.broadcasted_iota(jnp.int32, (1, PAIR), 1)
    halves = (lane < hd_w, lane >= hd_w)

    def gate_stats(ci, m_prev):
        g = g_ref[0, ci * tl:(ci + 1) * tl, :] * LOG2_E
        b_all = sum(_dot(tri, part) for part in _split3(g))
        a_cols = g - pltpu.roll(b_all, LANES - nh, axis=1)
        b_rows = b_all.T[nh:2 * nh]
        a_rows = g.T[0:nh] - b_rows
        b_last = b_rows[:, tl - 1:tl]
        log_w = b_last + a_rows
        m_new = jnp.maximum(b_last + m_prev, jnp.max(log_w, axis=1, keepdims=True))
        w_keys = jnp.exp2(log_w - m_new)
        return dict(a_cols=a_cols, b_rows=b_rows, log_inter=b_rows + m_prev, m_new=m_new,
                    w_keys=w_keys, decay=jnp.exp2(b_last + m_prev - m_new),
                    w_keys_b=jnp.concatenate([w_keys, jnp.zeros((pad - nh, tl), F32)],
                                             axis=0).astype(BF16))

    def chunk(ci, st):
        seq = slice(ci * tl, (ci + 1) * tl)
        h_rows = []
        for p in range(nh // 2):
            sl = slice(p * PAIR, (p + 1) * PAIR)
            qp = q_ref[0, seq, sl]
            kp = k_ref[0, seq, sl]
            vt = vt_ref[0, ci, sl, :]
            cn = _dot_nt(state[p][...].astype(BF16), qp)
            for hh in range(2):
                hd = 2 * p + hh
                rows = slice(hh * hd_w, (hh + 1) * hd_w)
                log_inter = st["log_inter"][hd:hd + 1]
                log_d = jnp.where(visible, st["a_cols"][:, hd:hd + 1] + st["b_rows"][hd:hd + 1], NEG)
                m_t = jnp.maximum(log_inter, jnp.max(log_d, axis=0, keepdims=True))
                w_inter = jnp.exp2(log_inter - m_t)
                q_h = jnp.where(halves[hh], qp, jnp.zeros_like(qp))
                s_t = (_dot_nt(kp, q_h) * jnp.exp2(log_d - m_t)).astype(BF16)
                nd = _dot(jnp.concatenate([vt[rows], ones_rows], axis=0), s_t)
                num = nd[0:hd_w] + w_inter * cn[rows]
                den = nd[hd_w:hd_w + 1] + w_inter * cn[PAIR + hh:PAIR + hh + 1]
                h_t = num / jnp.maximum(jnp.abs(den), jnp.exp2(-m_t))
                ms = jnp.mean(h_t * h_t, axis=0, keepdims=True)
                h_rows.append(h_t * lax.rsqrt(ms + EPS))
            vw = [(vt[hh * hd_w:(hh + 1) * hd_w].astype(F32)
                   * st["w_keys"][2 * p + hh:2 * p + hh + 1]).astype(BF16) for hh in range(2)]
            upd = _dot(jnp.concatenate(vw + [st["w_keys_b"]], axis=0), kp)
            for hh in range(2):
                hd = 2 * p + hh
                rows = slice(hh * hd_w, (hh + 1) * hd_w)
                d = st["decay"][hd:hd + 1]
                state[p][rows, :] = d * state[p][rows, :] + jnp.where(halves[hh], upd[rows], 0.0)
                state[p][PAIR + hh:PAIR + hh + 1, :] = (
                    d * state[p][PAIR + hh:PAIR + hh + 1, :]
                    + jnp.where(halves[hh], upd[PAIR + hd:PAIR + hd + 1], 0.0))
        o_ref[0, seq, :] = jnp.concatenate(h_rows, axis=0).T * nw_ref[...]

    stats = []
    m_run = m_st[:, 0:1]
    for ci in range(nc):
        stats.append(gate_stats(ci, m_run))
        m_run = stats[-1]["m_new"]
    m_st[...] = jnp.broadcast_to(m_run, m_st.shape)
    for ci in range(nc):
        chunk(ci, stats[ci])


def _mlstm(qm, km, vmt, gates, head_nw, *, tl, nc):
    bsz, s_len, w = qm.shape
    row = lambda width: pl.BlockSpec((1, nc * tl, width), lambda b, s: (b, s, 0))
    n_pairs = MLSTM_HEADS // 2
    return pl.pallas_call(
        functools.partial(_mlstm_kernel, tl=tl, nc=nc),
        grid=(bsz, s_len // (nc * tl)),
        in_specs=[row(w), row(w), pl.BlockSpec((1, nc, w, tl), lambda b, s: (b, s, 0, 0)),
                  row(LANES), _resident(head_nw.shape)],
        out_specs=row(w),
        out_shape=jax.ShapeDtypeStruct((bsz, s_len, w), F32),
        scratch_shapes=[pltpu.VMEM((MLSTM_HEADS, LANES), F32)]
        + [pltpu.VMEM((PAIR + 2 * SUBLANES, PAIR), F32)] * n_pairs,
        compiler_params=_params(2),
        name="mlstm",
    )(qm, km, vmt, gates, head_nw)


def _mix_out_kernel(x_ref, mod_ref, nw_ref, wc_ref, ya_ref, hm_ref, wba_ref, wbm_ref, wo_ref,
                    o_ref):
    mod = mod_ref[0]
    x = x_ref[0]
    hb = (_rms(x) * (nw_ref[...] * (1.0 + mod[1:2])) + mod[0:1]).astype(BF16)
    o_gate = jax.nn.sigmoid(_dot(hb, wc_ref[:, C_O:C_O + MLSTM_W]))
    y_mlstm = (o_gate * hm_ref[0]).astype(BF16)
    merged = jax.nn.sigmoid(_dot(hb, wc_ref[:, C_GA:C_GA + D_MODEL])) * _dot(ya_ref[0], wba_ref[...])
    merged = merged + (jax.nn.sigmoid(_dot(hb, wc_ref[:, C_GM:C_GM + D_MODEL]))
                       * _dot(y_mlstm, wbm_ref[...]))
    o_ref[0] = x + mod[2:3] * _dot(merged.astype(BF16), wo_ref[...])


def _mix_out(x, mod, nw, w_c, y_mla, hm, wba, wbm, wo, *, tm):
    bsz, s_len, d = x.shape
    row = lambda w: pl.BlockSpec((1, tm, w), lambda b, s: (b, s, 0))
    return pl.pallas_call(
        _mix_out_kernel,
        grid=(bsz, s_len // tm),
        in_specs=[row(d), pl.BlockSpec((1, 6, d), lambda b, s: (b, 0, 0)),
                  _resident(nw.shape), _resident(w_c.shape), row(MLA_W), row(MLSTM_W),
                  _resident(wba.shape), _resident(wbm.shape), _resident(wo.shape)],
        out_specs=row(d),
        out_shape=jax.ShapeDtypeStruct((bsz, s_len, d), F32),
        compiler_params=_params(2),
        name="mix_out",
    )(x, mod, nw, w_c, y_mla, hm, wba, wbm, wo)


def _gelu_tanh(a):
    k1 = -2.0 * 0.7978845608028654 * LOG2_E
    k2 = k1 * 0.044715
    return a / (1.0 + jnp.exp2(a * (k1 + k2 * (a * a))))


def _ffn_kernel(x_ref, mod_ref, nw_ref, wup_ref, cw_ref, cb_ref, wdn_ref, fnw_ref, o_ref,
                halo_ref, acc_ref, *ubuf, tm, final_norm):
    mod = mod_ref[0]
    x = x_ref[0]
    hb = (_rms(x) * (nw_ref[...] * (1.0 + mod[4:5])) + mod[3:4]).astype(BF16)

    @pl.when(pl.program_id(1) == 0)
    def _():
        halo_ref[...] = jnp.zeros(halo_ref.shape, F32)

    def cols(c, half):
        lo = half * D_FF + c * FF_CHUNK
        return slice(lo, lo + FF_CHUNK)

    strips = FF_CHUNK // LANES
    top = SUBLANES

    def up(c):
        for half in range(2):
            sl = cols(c, half)
            u = _dot(hb, wup_ref[:, sl])
            buf = ubuf[2 * (c % FF_AHEAD) + half]
            for k in range(strips):
                lanes = slice(k * LANES, (k + 1) * LANES)
                buf[k, 0:top, :] = halo_ref[:, sl][:, lanes]
                buf[k, top:top + tm, :] = u[:, lanes]
            halo_ref[:, sl] = u[tm - SUBLANES:tm, :]

    def conv_strip(c, half, k):
        lo = half * D_FF + c * FF_CHUNK + k * LANES
        sl = slice(lo, lo + LANES)
        buf = ubuf[2 * (c % FF_AHEAD) + half]
        y = cb_ref[:, sl] + cw_ref[FFN_CONV - 1:FFN_CONV, sl] * buf[k, top:top + tm, :]
        for j in range(1, FFN_CONV):
            y = y + cw_ref[FFN_CONV - 1 - j:FFN_CONV - j, sl] * buf[k, top - j:top - j + tm, :]
        return y

    def down(c):
        gated = [_gelu_tanh(conv_strip(c, 0, k)) * conv_strip(c, 1, k) for k in range(strips)]
        gated = jnp.concatenate(gated, axis=1).astype(BF16)
        return _dot(gated, wdn_ref[c * FF_CHUNK:(c + 1) * FF_CHUNK, :])

    for c in range(FF_AHEAD - 1):
        up(c)
    pending = None
    for c in range(N_FF):
        if c + FF_AHEAD - 1 < N_FF:
            up(c + FF_AHEAD - 1)
        part = down(c)
        if c % 2 == 0 and c + 1 < N_FF:
            pending = part
            continue
        if pending is not None:
            part = pending + part
            pending = None
        if c <= 1:
            acc_ref[...] = part
        else:
            acc_ref[...] += part
    y = x + mod[5:6] * acc_ref[...]
    if final_norm:
        y = _rms(y) * fnw_ref[...]
    o_ref[0] = y


def _ffn(x, mod, nw, wup, cw, cb, wdn, fnw, *, tm, final_norm):
    bsz, s_len, d = x.shape
    row = lambda w: pl.BlockSpec((1, tm, w), lambda b, s: (b, s, 0))
    return pl.pallas_call(
        functools.partial(_ffn_kernel, tm=tm, final_norm=final_norm),
        grid=(bsz, s_len // tm),
        in_specs=[row(d), pl.BlockSpec((1, 6, d), lambda b, s: (b, 0, 0)),
                  _resident(nw.shape), _resident(wup.shape), _resident(cw.shape),
                  _resident(cb.shape), _resident(wdn.shape), _resident(fnw.shape)],
        out_specs=row(d),
        out_shape=jax.ShapeDtypeStruct((bsz, s_len, d), F32),
        scratch_shapes=[pltpu.VMEM((SUBLANES, 2 * D_FF), F32), pltpu.VMEM((tm, d), F32)]
        + [pltpu.VMEM((FF_CHUNK // LANES, SUBLANES + tm, LANES), F32)] * (2 * FF_AHEAD),
        compiler_params=_params(2),
        name="conv_ffn",
    )(x, mod, nw, wup, cw, cb, wdn, fnw)


def _rot_cols(w):
    half = w.shape[-1] // 2
    return jnp.concatenate([-w[..., half:], w[..., :half]], axis=-1)


def _pad_cols(w, before, total):
    pad = [(0, 0)] * (w.ndim - 1) + [(before, total - before - w.shape[-1])]
    return jnp.pad(w, pad)


def _layer_weights(w_in, w_uq, w_ukv, gate_b):
    d = w_in.shape[0]
    o = 0
    seg = {}
    for name, width in (("cq", Q_LORA), ("ckv", KV_LORA), ("kr", QK_ROPE), ("qm", MLSTM_W),
                        ("km", MLSTM_W), ("vm", MLSTM_W), ("om", MLSTM_W), ("im", MLSTM_HEADS),
                        ("fm", MLSTM_HEADS), ("ga", D_MODEL), ("gm", D_MODEL)):
        seg[name] = w_in[:, o:o + width]
        o += width
    gates = _pad_cols(jnp.concatenate([seg["im"], seg["fm"]], axis=1), 0, LANES)
    w_a = jnp.concatenate([seg["cq"], seg["ckv"],
                           _pad_cols(seg["kr"], QK_NOPE, LANES),
                           _pad_cols(_rot_cols(seg["kr"]), QK_NOPE, LANES),
                           seg["qm"], seg["km"], gates], axis=1).astype(BF16)
    wvm = seg["vm"].T.astype(BF16)
    w_c = jnp.concatenate([seg["om"], seg["ga"], seg["gm"]], axis=1).astype(BF16)
    uq = w_uq.reshape(Q_LORA, MLA_HEADS, QK_NOPE + QK_ROPE)
    wuqa = _pad_cols(uq, 0, HEAD_SLAB).reshape(Q_LORA, -1).astype(BF16)
    wuqb = _pad_cols(_rot_cols(uq[..., QK_NOPE:]), QK_NOPE, HEAD_SLAB).reshape(Q_LORA, -1).astype(BF16)
    ukv = w_ukv.reshape(KV_LORA, MLA_HEADS, QK_NOPE + V_HEAD)
    wuk = _pad_cols(ukv[..., :QK_NOPE], 0, HEAD_SLAB).reshape(KV_LORA, -1).astype(BF16)
    wuv = _pad_cols(ukv[..., QK_NOPE:], 0, HEAD_SLAB).reshape(KV_LORA, -1).T.astype(BF16)
    gb = _pad_cols(gate_b.reshape(1, -1), 0, LANES)
    assert w_a.shape == (d, A_COLS) and w_c.shape == (d, C_COLS)
    return w_a, w_c, wuqa, wuqb, wuk, wuv, wvm, gb


def kernel(x, c, positions, ada_w, ada_b, norm_mix_w, w_in, q_norm_w, kv_norm_w, w_uq, w_ukv,
           mlstm_conv_w, mlstm_conv_b, mlstm_gate_b, mlstm_head_norm_w, w_br_mla, w_br_mlstm,
           w_out, norm_ffn_w, ffn_w_up, ffn_conv_w, ffn_conv_b, ffn_w_down, final_norm_w):
    bsz, s_len, d = x.shape
    depth = ada_w.shape[0]
    tm = min(1024, s_len)
    tf = min(512, s_len)
    tq = min(512, s_len)
    tl = min(256, s_len)
    nc = max(n for n in (1, 2, 4) if s_len % (n * tl) == 0)
    assert all(s_len % t == 0 for t in (tm, tf, tq, tl)) and tm % tq == 0 and tm % tl == 0, s_len

    mod_all = _modulation(c, ada_w, ada_b).reshape(depth, bsz, 6, d)
    cos, sin = _rope_tables(positions)
    ones = jnp.ones((bsz, s_len, QK_NOPE), F32)
    zeros = jnp.zeros((bsz, s_len, HEAD_SLAB - QK_NOPE - QK_ROPE), F32)
    cosp = jnp.concatenate([ones, cos, cos, zeros], axis=-1)
    sinp = jnp.concatenate([jnp.zeros_like(ones), sin, sin, zeros], axis=-1)
    row = lambda a: a.reshape(1, -1)

    for l in range(depth):
        mod = mod_all[l]
        w_a, w_c, wuqa, wuqb, wuk, wuv, wvm, gb = _layer_weights(w_in[l], w_uq[l], w_ukv[l],
                                                                mlstm_gate_b[l])
        q, k, v, qm, km, vm, gates = _in_proj(
            x, mod, row(norm_mix_w[l]), w_a, row(q_norm_w[l]), row(kv_norm_w[l]), wuqa, wuqb,
            wuk, wuv, wvm, mlstm_conv_w[l], row(mlstm_conv_b[l]), gb, cosp, sinp, tm=tm, tq=tq, tl=tl)
        y_mla = _attention(q, k, v, tq=tq)
        hm = _mlstm(qm, km, vm, gates, row(mlstm_head_norm_w[l]), tl=tl, nc=nc)
        x = _mix_out(x, mod, row(norm_mix_w[l]), w_c, y_mla, hm, w_br_mla[l].astype(BF16),
                     w_br_mlstm[l].astype(BF16), w_out[l].astype(BF16), tm=tm)
        x = _ffn(x, mod, row(norm_ffn_w[l]), ffn_w_up[l].astype(BF16), ffn_conv_w[l],
                 row(ffn_conv_b[l]), ffn_w_down[l].astype(BF16), row(final_norm_w), tm=tf,
                 final_norm=(l == depth - 1))
    return x
```

```python
import functools

import jax
import jax.numpy as jnp
from jax import lax
from jax.experimental import pallas as pl
from jax.experimental.pallas import tpu as pltpu

F32 = jnp.float32
BF16 = jnp.bfloat16

D_MODEL = 1024
CHUNK = 64
MLA_HEADS = 8
Q_LORA = 384
KV_LORA = 256
QK_NOPE = 64
QK_ROPE = 32
V_HEAD = 64
ROPE_THETA = 10000.0
MLSTM_HEADS = 8
MLSTM_HEAD_DIM = 64
MLSTM_W = MLSTM_HEADS * MLSTM_HEAD_DIM
MLSTM_CONV = 4
D_FF = 2816
FFN_CONV = 3
MLA_W = MLA_HEADS * V_HEAD
EPS = 1e-6

LANES = 128
SUBLANES = 8
HEAD_SLAB = 128
PAIR = 2 * MLSTM_HEAD_DIM
NEG = -1e30
LOG2_E = 1.4426950408889634

A_CQ = 0
A_CKV = A_CQ + Q_LORA
A_KR = A_CKV + KV_LORA
A_KRR = A_KR + LANES
A_QK = A_KRR + LANES
A_G = A_QK + 2 * MLSTM_W
A_COLS = A_G + LANES
C_O = 0
C_GA = C_O + MLSTM_W
C_GM = C_GA + D_MODEL
C_COLS = C_GM + D_MODEL

FF_CHUNK = 256
N_FF = D_FF // FF_CHUNK
FF_AHEAD = 6
ATTN_HEADS = 8

VMEM_LIMIT = 56 * 1024 * 1024


def _params(n_axes):
    return pltpu.CompilerParams(dimension_semantics=("arbitrary",) * n_axes,
                                vmem_limit_bytes=VMEM_LIMIT)


def _resident(shape):
    nd = len(shape)
    return pl.BlockSpec(shape, lambda *_: (0,) * nd, pipeline_mode=pl.Buffered(1))


def _rms(x):
    return x * lax.rsqrt(jnp.mean(x * x, axis=-1, keepdims=True) + EPS)


def _dot(a, b):
    return jnp.dot(a, b, preferred_element_type=F32)


def _dot_nt(a, b):
    return lax.dot_general(a, b, (((1,), (1,)), ((), ())), preferred_element_type=F32)


def _mod_kernel(c_ref, w_ref, b_ref, o_ref):
    c = c_ref[...]
    c_act = (c * jax.nn.sigmoid(c)).astype(BF16)
    o_ref[0] = _dot(c_act, w_ref[0].astype(BF16)) + b_ref[0]


def _modulation(c, ada_w, ada_b):
    depth, d, n = ada_w.shape
    bsz = c.shape[0]
    tn = 1024
    return pl.pallas_call(
        _mod_kernel,
        grid=(depth, n // tn),
        in_specs=[pl.BlockSpec((bsz, d), lambda l, j: (0, 0)),
                  pl.BlockSpec((1, d, tn), lambda l, j: (l, 0, j)),
                  pl.BlockSpec((1, 1, tn), lambda l, j: (l, 0, j))],
        out_specs=pl.BlockSpec((1, bsz, tn), lambda l, j: (l, 0, j)),
        out_shape=jax.ShapeDtypeStruct((depth, bsz, n), F32),
        compiler_params=_params(2),
        name="modulation",
    )(c, ada_w, ada_b.reshape(depth, 1, n))


def _rope_kernel(pos_ref, inv_ref, cos_ref, sin_ref):
    ang = pos_ref[...].astype(F32) * inv_ref[...]
    cos_ref[...] = jnp.cos(ang)
    sin_ref[...] = jnp.sin(ang)


def _rope_tables(positions):
    bsz, s_len = positions.shape
    half = QK_ROPE // 2
    per_row = LANES // half
    rows = bsz * s_len // per_row
    pos = jnp.repeat(positions.reshape(rows, per_row), half, axis=1)
    inv = ROPE_THETA ** (-jnp.arange(half, dtype=F32) / half)
    inv = jnp.tile(inv, per_row).reshape(1, LANES)
    tr = min(rows, 512)
    cos, sin = pl.pallas_call(
        _rope_kernel,
        grid=(rows // tr,),
        in_specs=[pl.BlockSpec((tr, LANES), lambda i: (i, 0)),
                  pl.BlockSpec((1, LANES), lambda i: (0, 0))],
        out_specs=[pl.BlockSpec((tr, LANES), lambda i: (i, 0))] * 2,
        out_shape=[jax.ShapeDtypeStruct((rows, LANES), F32)] * 2,
        compiler_params=_params(1),
        name="rope_tables",
    )(pos, inv)
    return cos.reshape(bsz, s_len, half), sin.reshape(bsz, s_len, half)


def _in_proj_kernel(x_ref, mod_ref, nw_ref, win_ref, qnw_ref, kvnw_ref, wuqa_ref, wuqb_ref,
                    wuk_ref, wuv_ref, wvm_ref, cw_ref, cb_ref, gb_ref, cos_ref, sin_ref,
                    q_ref, k_ref, v_ref, qm_ref, km_ref, vm_ref, g_ref, halo_ref, qk_buf, *, tm, tq, tl):
    mod = mod_ref[0]
    h = _rms(x_ref[0]) * (nw_ref[...] * (1.0 + mod[1:2])) + mod[0:1]
    hb = h.astype(BF16)
    cos = cos_ref[0]
    sin = sin_ref[0]

    def proj(lo, width):
        return _dot(hb, win_ref[:, lo:lo + width])

    @pl.when(pl.program_id(1) == 0)
    def _():
        halo_ref[...] = jnp.zeros(halo_ref.shape, F32)

    cq = proj(A_CQ, Q_LORA)
    ckv = proj(A_CKV, KV_LORA)
    kr = proj(A_KR, LANES) * cos + proj(A_KRR, LANES) * sin
    g = proj(A_G, LANES) + gb_ref[...]
    qk = proj(A_QK, 2 * MLSTM_W)
    top = SUBLANES
    for s in range(2 * MLSTM_W // LANES):
        lanes = slice(s * LANES, (s + 1) * LANES)
        qk_buf[s, 0:top, :] = halo_ref[:, lanes]
        qk_buf[s, top:top + tm, :] = qk[:, lanes]
    halo_ref[...] = qk[tm - SUBLANES:tm, :]
    vmt = _dot_nt(wvm_ref[...], hb).astype(BF16)
    for c in range(tm // tl):
        vm_ref[0, c] = vmt[:, c * tl:(c + 1) * tl]

    cqn = (_rms(cq) * qnw_ref[...]).astype(BF16)
    ckvn = (_rms(ckv) * kvnw_ref[...]).astype(BF16)

    logf = jnp.minimum(g, 0.0) - jnp.log1p(jnp.exp(-jnp.abs(g)))
    lane = lax.broadcasted_iota(jnp.int32, g.shape, 1)
    g_ref[0] = jnp.where(lane < MLSTM_HEADS, g, logf)

    slab_row = lax.broadcasted_iota(jnp.int32, (MLA_HEADS * HEAD_SLAB, 1), 0) % HEAD_SLAB
    vt = jnp.where(slab_row == V_HEAD, 1.0, _dot_nt(wuv_ref[...], ckvn)).astype(BF16)
    for c in range(tm // tq):
        v_ref[0, c] = vt[:, c * tq:(c + 1) * tq]
    two = 2 * HEAD_SLAB
    for hp in range(MLA_HEADS // 2):
        kn = _dot(ckvn, wuk_ref[:, hp * two:(hp + 1) * two])
        for hh in range(2):
            sl = slice(hh * HEAD_SLAB, (hh + 1) * HEAD_SLAB)
            k_ref[0, :, hp * two + hh * HEAD_SLAB:hp * two + (hh + 1) * HEAD_SLAB] = \
                (kn[:, sl] + kr).astype(BF16)

    scale = (QK_NOPE + QK_ROPE) ** -0.5 * LOG2_E
    n_groups = MLA_HEADS // 2
    strips_per_group = 2 * MLSTM_W // LANES // n_groups
    for hp in range(n_groups):
        qa = _dot(cqn, wuqa_ref[:, hp * two:(hp + 1) * two])
        qb = _dot(cqn, wuqb_ref[:, hp * two:(hp + 1) * two])
        for hh in range(2):
            sl = slice(hh * HEAD_SLAB, (hh + 1) * HEAD_SLAB)
            q_ref[0, :, hp * two + hh * HEAD_SLAB:hp * two + (hh + 1) * HEAD_SLAB] = \
                ((qa[:, sl] * cos + qb[:, sl] * sin) * scale).astype(BF16)
        for s in range(hp * strips_per_group, (hp + 1) * strips_per_group):
            lanes = slice(s * LANES, (s + 1) * LANES)
            y = cb_ref[:, lanes] + cw_ref[MLSTM_CONV - 1:MLSTM_CONV, lanes] * qk_buf[s, top:top + tm, :]
            for j in range(1, MLSTM_CONV):
                y = y + (cw_ref[MLSTM_CONV - 1 - j:MLSTM_CONV - j, lanes]
                         * qk_buf[s, top - j:top - j + tm, :])
            y = y * jax.nn.sigmoid(y)
            if s * LANES < MLSTM_W:
                qm_ref[0, :, lanes] = y.astype(BF16)
            else:
                km_ref[0, :, s * LANES - MLSTM_W:(s + 1) * LANES - MLSTM_W] = \
                    (y * (MLSTM_HEAD_DIM ** -0.5)).astype(BF16)


def _in_proj(x, mod, nw, w_a, qnw, kvnw, wuqa, wuqb, wuk, wuv, wvm, cw, cb, gb, cosp, sinp, *, tm, tq, tl):
    bsz, s_len, d = x.shape
    row = lambda w: pl.BlockSpec((1, tm, w), lambda b, s: (b, s, 0))
    slabs = MLA_HEADS * HEAD_SLAB
    out_widths = (slabs, slabs, LANES, MLSTM_W, MLSTM_W, LANES, LANES)
    out_shape = [jax.ShapeDtypeStruct((bsz, s_len, w), BF16) for w in out_widths]
    out_specs = [row(w) for w in out_widths]
    out_shape[2] = jax.ShapeDtypeStruct((bsz, s_len // tq, slabs, tq), BF16)
    out_specs[2] = pl.BlockSpec((1, tm // tq, slabs, tq), lambda b, s: (b, s, 0, 0))
    out_shape[5] = jax.ShapeDtypeStruct((bsz, s_len // tl, MLSTM_W, tl), BF16)
    out_specs[5] = pl.BlockSpec((1, tm // tl, MLSTM_W, tl), lambda b, s: (b, s, 0, 0))
    out_shape[6] = jax.ShapeDtypeStruct((bsz, s_len, LANES), F32)
    out_specs[6] = row(LANES)
    return pl.pallas_call(
        functools.partial(_in_proj_kernel, tm=tm, tq=tq, tl=tl),
        grid=(bsz, s_len // tm),
        in_specs=[row(d),
                  pl.BlockSpec((1, 6, d), lambda b, s: (b, 0, 0)),
                  _resident(nw.shape), _resident(w_a.shape), _resident(qnw.shape),
                  _resident(kvnw.shape), _resident(wuqa.shape), _resident(wuqb.shape),
                  _resident(wuk.shape), _resident(wuv.shape), _resident(wvm.shape),
                  _resident(cw.shape),
                  _resident(cb.shape), _resident(gb.shape), row(LANES), row(LANES)],
        out_specs=out_specs,
        out_shape=out_shape,
        scratch_shapes=[pltpu.VMEM((SUBLANES, 2 * MLSTM_W), F32),
                        pltpu.VMEM((2 * MLSTM_W // LANES, SUBLANES + tm, LANES), F32)],
        compiler_params=_params(2),
        name="in_proj",
    )(x, mod, nw, w_a, qnw, kvnw, wuqa, wuqb, wuk, wuv, wvm, cw, cb, gb, cosp, sinp)


def _attn_kernel(q_ref, k_ref, vt_ref, o_ref, *scratch, tq):
    nh = ATTN_HEADS
    qi = pl.program_id(2)
    half = tq // 2
    heads = [slice(hh * HEAD_SLAB, (hh + 1) * HEAD_SLAB) for hh in range(nh)]
    s_buf, mx_buf, m_ref, acc_ref = (scratch[i::4] for i in range(4))

    def chunk_visible(n_keys, n_queries):
        key_chunk = lax.broadcasted_iota(jnp.int32, (n_keys, 1), 0) // CHUNK
        qry_chunk = lax.broadcasted_iota(jnp.int32, (1, n_queries), 1) // CHUNK
        return key_chunk <= qry_chunk

    def scores(j, hh, diagonal):
        start = pl.multiple_of(j * tq, tq)
        if not diagonal:
            st = _dot_nt(k_ref[0, pl.ds(start, tq), heads[hh]], q_ref[0, :, heads[hh]])
            s_buf[hh][...] = st
            mx_buf[hh][...] = jnp.max(st, axis=0, keepdims=True)
            return
        sa = _dot_nt(k_ref[0, pl.ds(start, half), heads[hh]], q_ref[0, :, heads[hh]])
        sa = jnp.where(chunk_visible(half, tq), sa, NEG)
        sb = _dot_nt(k_ref[0, pl.ds(start + half, half), heads[hh]], q_ref[0, half:tq, heads[hh]])
        sb = jnp.where(chunk_visible(half, half), sb, NEG)
        s_buf[hh][0:half, :] = sa
        s_buf[hh][half:tq, half:tq] = sb
        ma = jnp.max(sa, axis=0, keepdims=True)
        mx_buf[hh][:, 0:half] = ma[:, 0:half]
        mx_buf[hh][:, half:tq] = jnp.maximum(ma[:, half:tq], jnp.max(sb, axis=0, keepdims=True))

    def accumulate(j, hh, diagonal):
        m_old = m_ref[hh][...]
        m_new = jnp.maximum(m_old, mx_buf[hh][...])
        alpha = jnp.exp2(m_old - m_new)
        if not diagonal:
            p = jnp.exp2(s_buf[hh][...] - m_new).astype(BF16)
            acc_ref[hh][...] = alpha * acc_ref[hh][...] + _dot(vt_ref[0, j, heads[hh], :], p)
        else:
            pa = jnp.exp2(s_buf[hh][0:half, :] - m_new).astype(BF16)
            pb = jnp.exp2(s_buf[hh][half:tq, half:tq] - m_new[:, half:tq]).astype(BF16)
            acc_ref[hh][...] = alpha * acc_ref[hh][...] + _dot(vt_ref[0, j, heads[hh], 0:half], pa)
            acc_ref[hh][:, half:tq] += _dot(vt_ref[0, j, heads[hh], half:tq], pb)
        m_ref[hh][...] = m_new

    def block(j, diagonal, next_diagonal):
        for hh in range(nh):
            if hh + 1 < nh:
                scores(j, hh + 1, diagonal)
            elif next_diagonal is not None:
                scores(j + 1, 0, next_diagonal)
            accumulate(j, hh, diagonal)

    for hh in range(nh):
        m_ref[hh][...] = jnp.full(m_ref[hh].shape, NEG, F32)
        acc_ref[hh][...] = jnp.zeros(acc_ref[hh].shape, F32)

    @pl.when(qi == 0)
    def _():
        scores(0, 0, True)

    @pl.when(qi > 0)
    def _():
        scores(0, 0, False)

        def body(j, carry):
            block(j, False, False)
            return carry

        lax.fori_loop(0, qi - 1, body, 0)
        block(qi - 1, False, True)

    block(qi, True, None)
    outs = [acc_ref[hh][0:V_HEAD, :] / acc_ref[hh][V_HEAD:V_HEAD + 1, :] for hh in range(nh)]
    o_ref[0] = jnp.concatenate(outs, axis=0).T.astype(BF16)


def _attention(q, k, vt, *, tq):
    bsz, s_len, _ = q.shape
    width = ATTN_HEADS * HEAD_SLAB
    n_kv = s_len // tq
    return pl.pallas_call(
        functools.partial(_attn_kernel, tq=tq),
        grid=(bsz, MLA_HEADS // ATTN_HEADS, n_kv),
        in_specs=[pl.BlockSpec((1, tq, width), lambda b, h, i: (b, i, h)),
                  pl.BlockSpec((1, s_len, width), lambda b, h, i: (b, 0, h)),
                  pl.BlockSpec((1, n_kv, width, tq), lambda b, h, i: (b, 0, h, 0))],
        out_specs=pl.BlockSpec((1, tq, ATTN_HEADS * V_HEAD), lambda b, h, i: (b, i, h)),
        out_shape=jax.ShapeDtypeStruct((bsz, s_len, MLA_W), BF16),
        scratch_shapes=[pltpu.VMEM((tq, tq), F32), pltpu.VMEM((1, tq), F32),
                        pltpu.VMEM((1, tq), F32), pltpu.VMEM((HEAD_SLAB, tq), F32)] * ATTN_HEADS,
        compiler_params=_params(3),
        name="mla_attention",
    )(q, k, vt)


def _split3(a):
    hi = a.astype(BF16)
    r = a - hi.astype(F32)
    mid = r.astype(BF16)
    lo = (r - mid.astype(F32)).astype(BF16)
    return hi, mid, lo


def _mlstm_kernel(q_ref, k_ref, vt_ref, g_ref, nw_ref, o_ref, m_st, *state, tl, nc):
    @pl.when(pl.program_id(1) == 0)
    def _():
        for ref in state + (m_st,):
            ref[...] = jnp.zeros(ref.shape, F32)

    nh, hd_w = MLSTM_HEADS, MLSTM_HEAD_DIM
    pad = 2 * SUBLANES
    key = lax.broadcasted_iota(jnp.int32, (tl, 1), 0)
    qry = lax.broadcasted_iota(jnp.int32, (1, tl), 1)
    visible = key <= qry
    tri = jnp.where(qry <= key, 1.0, 0.0).astype(BF16)
    ones_rows = jnp.ones((pad, tl), BF16)
    lane = lax.broadcasted_iota(jnp.int32, (1, PAIR), 1)
    halves = (lane < hd_w, lane >= hd_w)

    def gate_stats(ci, m_prev):
        g = g_ref[0, ci * tl:(ci + 1) * tl, :] * LOG2_E
        b_all = sum(_dot(tri, part) for part in _split3(g))
        a_cols = g - pltpu.roll(b_all, LANES - nh, axis=1)
        b_rows = b_all.T[nh:2 * nh]
        a_rows = g.T[0:nh] - b_rows
        b_last = b_rows[:, tl - 1:tl]
        log_w = b_last + a_rows
        m_new = jnp.maximum(b_last + m_prev, jnp.max(log_w, axis=1, keepdims=True))
        w_keys = jnp.exp2(log_w - m_new)
        return dict(a_cols=a_cols, b_rows=b_rows, log_inter=b_rows + m_prev, m_new=m_new,
                    w_keys=w_keys, decay=jnp.exp2(b_last + m_prev - m_new),
                    w_keys_b=jnp.concatenate([w_keys, jnp.zeros((pad - nh, tl), F32)],
                                             axis=0).astype(BF16))

    def chunk(ci, st):
        seq = slice(ci * tl, (ci + 1) * tl)
        h_rows = []
        for p in range(nh // 2):
            sl = slice(p * PAIR, (p + 1) * PAIR)
            qp = q_ref[0, seq, sl]
            kp = k_ref[0, seq, sl]
            vt = vt_ref[0, ci, sl, :]
            cn = _dot_nt(state[p][...].astype(BF16), qp)
            for hh in range(2):
                hd = 2 * p + hh
                rows = slice(hh * hd_w, (hh + 1) * hd_w)
                log_inter = st["log_inter"][hd:hd + 1]
                log_d = jnp.where(visible, st["a_cols"][:, hd:hd + 1] + st["b_rows"][hd:hd + 1], NEG)
                m_t = jnp.maximum(log_inter, jnp.max(log_d, axis=0, keepdims=True))
                w_inter = jnp.exp2(log_inter - m_t)
                q_h = jnp.where(halves[hh], qp, jnp.zeros_like(qp))
                s_t = (_dot_nt(kp, q_h) * jnp.exp2(log_d - m_t)).astype(BF16)
                nd = _dot(jnp.concatenate([vt[rows], ones_rows], axis=0), s_t)
                num = nd[0:hd_w] + w_inter * cn[rows]
                den = nd[hd_w:hd_w + 1] + w_inter * cn[PAIR + hh:PAIR + hh + 1]
                h_t = num / jnp.maximum(jnp.abs(den), jnp.exp2(-m_t))
                ms = jnp.mean(h_t * h_t, axis=0, keepdims=True)
                h_rows.append(h_t * lax.rsqrt(ms + EPS))
            vw = [(vt[hh * hd_w:(hh + 1) * hd_w].astype(F32)
                   * st["w_keys"][2 * p + hh:2 * p + hh + 1]).astype(BF16) for hh in range(2)]
            upd = _dot(jnp.concatenate(vw + [st["w_keys_b"]], axis=0), kp)
            for hh in range(2):
                hd = 2 * p + hh
                rows = slice(hh * hd_w, (hh + 1) * hd_w)
                d = st["decay"][hd:hd + 1]
                state[p][rows, :] = d * state[p][rows, :] + jnp.where(halves[hh], upd[rows], 0.0)
                state[p][PAIR + hh:PAIR + hh + 1, :] = (
                    d * state[p][PAIR + hh:PAIR + hh + 1, :]
                    + jnp.where(halves[hh], upd[PAIR + hd:PAIR + hd + 1], 0.0))
        o_ref[0, seq, :] = jnp.concatenate(h_rows, axis=0).T * nw_ref[...]

    stats = []
    m_run = m_st[:, 0:1]
    for ci in range(nc):
        stats.append(gate_stats(ci, m_run))
        m_run = stats[-1]["m_new"]
    m_st[...] = jnp.broadcast_to(m_run, m_st.shape)
    for ci in range(nc):
        chunk(ci, stats[ci])


def _mlstm(qm, km, vmt, gates, head_nw, *, tl, nc):
    bsz, s_len, w = qm.shape
    row = lambda width: pl.BlockSpec((1, nc * tl, width), lambda b, s: (b, s, 0))
    n_pairs = MLSTM_HEADS // 2
    return pl.pallas_call(
        functools.partial(_mlstm_kernel, tl=tl, nc=nc),
        grid=(bsz, s_len // (nc * tl)),
        in_specs=[row(w), row(w), pl.BlockSpec((1, nc, w, tl), lambda b, s: (b, s, 0, 0)),
                  row(LANES), _resident(head_nw.shape)],
        out_specs=row(w),
        out_shape=jax.ShapeDtypeStruct((bsz, s_len, w), F32),
        scratch_shapes=[pltpu.VMEM((MLSTM_HEADS, LANES), F32)]
        + [pltpu.VMEM((PAIR + 2 * SUBLANES, PAIR), F32)] * n_pairs,
        compiler_params=_params(2),
        name="mlstm",
    )(qm, km, vmt, gates, head_nw)


def _mix_out_kernel(x_ref, mod_ref, nw_ref, wc_ref, ya_ref, hm_ref, wba_ref, wbm_ref, wo_ref,
                    o_ref):
    mod = mod_ref[0]
    x = x_ref[0]
    hb = (_rms(x) * (nw_ref[...] * (1.0 + mod[1:2])) + mod[0:1]).astype(BF16)
    o_gate = jax.nn.sigmoid(_dot(hb, wc_ref[:, C_O:C_O + MLSTM_W]))
    y_mlstm = (o_gate * hm_ref[0]).astype(BF16)
    merged = jax.nn.sigmoid(_dot(hb, wc_ref[:, C_GA:C_GA + D_MODEL])) * _dot(ya_ref[0], wba_ref[...])
    merged = merged + (jax.nn.sigmoid(_dot(hb, wc_ref[:, C_GM:C_GM + D_MODEL]))
                       * _dot(y_mlstm, wbm_ref[...]))
    o_ref[0] = x + mod[2:3] * _dot(merged.astype(BF16), wo_ref[...])


def _mix_out(x, mod, nw, w_c, y_mla, hm, wba, wbm, wo, *, tm):
    bsz, s_len, d = x.shape
    row = lambda w: pl.BlockSpec((1, tm, w), lambda b, s: (b, s, 0))
    return pl.pallas_call(
        _mix_out_kernel,
        grid=(bsz, s_len // tm),
        in_specs=[row(d), pl.BlockSpec((1, 6, d), lambda b, s: (b, 0, 0)),
                  _resident(nw.shape), _resident(w_c.shape), row(MLA_W), row(MLSTM_W),
                  _resident(wba.shape), _resident(wbm.shape), _resident(wo.shape)],
        out_specs=row(d),
        out_shape=jax.ShapeDtypeStruct((bsz, s_len, d), F32),
        compiler_params=_params(2),
        name="mix_out",
    )(x, mod, nw, w_c, y_mla, hm, wba, wbm, wo)


def _gelu_tanh(a):
    k1 = -2.0 * 0.7978845608028654 * LOG2_E
    k2 = k1 * 0.044715
    return a / (1.0 + jnp.exp2(a * (k1 + k2 * (a * a))))


def _ffn_kernel(x_ref, mod_ref, nw_ref, wup_ref, cw_ref, cb_ref, wdn_ref, fnw_ref, o_ref,
                halo_ref, acc_ref, *ubuf, tm, final_norm):
    mod = mod_ref[0]
    x = x_ref[0]
    hb = (_rms(x) * (nw_ref[...] * (1.0 + mod[4:5])) + mod[3:4]).astype(BF16)

    @pl.when(pl.program_id(1) == 0)
    def _():
        halo_ref[...] = jnp.zeros(halo_ref.shape, F32)

    def cols(c, half):
        lo = half * D_FF + c * FF_CHUNK
        return slice(lo, lo + FF_CHUNK)

    strips = FF_CHUNK // LANES
    top = SUBLANES

    def up(c):
        for half in range(2):
            sl = cols(c, half)
            u = _dot(hb, wup_ref[:, sl])
            buf = ubuf[2 * (c % FF_AHEAD) + half]
            for k in range(strips):
                lanes = slice(k * LANES, (k + 1) * LANES)
                buf[k, 0:top, :] = halo_ref[:, sl][:, lanes]
                buf[k, top:top + tm, :] = u[:, lanes]
            halo_ref[:, sl] = u[tm - SUBLANES:tm, :]

    def conv_strip(c, half, k):
        lo = half * D_FF + c * FF_CHUNK + k * LANES
        sl = slice(lo, lo + LANES)
        buf = ubuf[2 * (c % FF_AHEAD) + half]
        y = cb_ref[:, sl] + cw_ref[FFN_CONV - 1:FFN_CONV, sl] * buf[k, top:top + tm, :]
        for j in range(1, FFN_CONV):
            y = y + cw_ref[FFN_CONV - 1 - j:FFN_CONV - j, sl] * buf[k, top - j:top - j + tm, :]
        return y

    def down(c):
        gated = [_gelu_tanh(conv_strip(c, 0, k)) * conv_strip(c, 1, k) for k in range(strips)]
        gated = jnp.concatenate(gated, axis=1).astype(BF16)
        return _dot(gated, wdn_ref[c * FF_CHUNK:(c + 1) * FF_CHUNK, :])

    for c in range(FF_AHEAD - 1):
        up(c)
    pending = None
    for c in range(N_FF):
        if c + FF_AHEAD - 1 < N_FF:
            up(c + FF_AHEAD - 1)
        part = down(c)
        if c % 2 == 0 and c + 1 < N_FF:
            pending = part
            continue
        if pending is not None:
            part = pending + part
            pending = None
        if c <= 1:
            acc_ref[...] = part
        else:
            acc_ref[...] += part
    y = x + mod[5:6] * acc_ref[...]
    if final_norm:
        y = _rms(y) * fnw_ref[...]
    o_ref[0] = y


def _ffn(x, mod, nw, wup, cw, cb, wdn, fnw, *, tm, final_norm):
    bsz, s_len, d = x.shape
    row = lambda w: pl.BlockSpec((1, tm, w), lambda b, s: (b, s, 0))
    return pl.pallas_call(
        functools.partial(_ffn_kernel, tm=tm, final_norm=final_norm),
        grid=(bsz, s_len // tm),
        in_specs=[row(d), pl.BlockSpec((1, 6, d), lambda b, s: (b, 0, 0)),
                  _resident(nw.shape), _resident(wup.shape), _resident(cw.shape),
                  _resident(cb.shape), _resident(wdn.shape), _resident(fnw.shape)],
        out_specs=row(d),
        out_shape=jax.ShapeDtypeStruct((bsz, s_len, d), F32),
        scratch_shapes=[pltpu.VMEM((SUBLANES, 2 * D_FF), F32), pltpu.VMEM((tm, d), F32)]
        + [pltpu.VMEM((FF_CHUNK // LANES, SUBLANES + tm, LANES), F32)] * (2 * FF_AHEAD),
        compiler_params=_params(2),
        name="conv_ffn",
    )(x, mod, nw, wup, cw, cb, wdn, fnw)


def _rot_cols(w):
    half = w.shape[-1] // 2
    return jnp.concatenate([-w[..., half:], w[..., :half]], axis=-1)


def _pad_cols(w, before, total):
    pad = [(0, 0)] * (w.ndim - 1) + [(before, total - before - w.shape[-1])]
    return jnp.pad(w, pad)


def _layer_weights(w_in, w_uq, w_ukv, gate_b):
    d = w_in.shape[0]
    o = 0
    seg = {}
    for name, width in (("cq", Q_LORA), ("ckv", KV_LORA), ("kr", QK_ROPE), ("qm", MLSTM_W),
                        ("km", MLSTM_W), ("vm", MLSTM_W), ("om", MLSTM_W), ("im", MLSTM_HEADS),
                        ("fm", MLSTM_HEADS), ("ga", D_MODEL), ("gm", D_MODEL)):
        seg[name] = w_in[:, o:o + width]
        o += width
    gates = _pad_cols(jnp.concatenate([seg["im"], seg["fm"]], axis=1), 0, LANES)
    w_a = jnp.concatenate([seg["cq"], seg["ckv"],
                           _pad_cols(seg["kr"], QK_NOPE, LANES),
                           _pad_cols(_rot_cols(seg["kr"]), QK_NOPE, LANES),
                           seg["qm"], seg["km"], gates], axis=1).astype(BF16)
    wvm = seg["vm"].T.astype(BF16)
    w_c = jnp.concatenate([seg["om"], seg["ga"], seg["gm"]], axis=1).astype(BF16)
    uq = w_uq.reshape(Q_LORA, MLA_HEADS, QK_NOPE + QK_ROPE)
    wuqa = _pad_cols(uq, 0, HEAD_SLAB).reshape(Q_LORA, -1).astype(BF16)
    wuqb = _pad_cols(_rot_cols(uq[..., QK_NOPE:]), QK_NOPE, HEAD_SLAB).reshape(Q_LORA, -1).astype(BF16)
    ukv = w_ukv.reshape(KV_LORA, MLA_HEADS, QK_NOPE + V_HEAD)
    wuk = _pad_cols(ukv[..., :QK_NOPE], 0, HEAD_SLAB).reshape(KV_LORA, -1).astype(BF16)
    wuv = _pad_cols(ukv[..., QK_NOPE:], 0, HEAD_SLAB).reshape(KV_LORA, -1).T.astype(BF16)
    gb = _pad_cols(gate_b.reshape(1, -1), 0, LANES)
    assert w_a.shape == (d, A_COLS) and w_c.shape == (d, C_COLS)
    return w_a, w_c, wuqa, wuqb, wuk, wuv, wvm, gb


def kernel(x, c, positions, ada_w, ada_b, norm_mix_w, w_in, q_norm_w, kv_norm_w, w_uq, w_ukv,
           mlstm_conv_w, mlstm_conv_b, mlstm_gate_b, mlstm_head_norm_w, w_br_mla, w_br_mlstm,
           w_out, norm_ffn_w, ffn_w_up, ffn_conv_w, ffn_conv_b, ffn_w_down, final_norm_w):
    bsz, s_len, d = x.shape
    depth = ada_w.shape[0]
    tm = min(1024, s_len)
    tf = min(512, s_len)
    tq = min(512, s_len)
    tl = min(256, s_len)
    nc = max(n for n in (1, 2, 4, 8) if s_len % (n * tl) == 0)
    assert all(s_len % t == 0 for t in (tm, tf, tq, tl)) and tm % tq == 0 and tm % tl == 0, s_len

    mod_all = _modulation(c, ada_w, ada_b).reshape(depth, bsz, 6, d)
    cos, sin = _rope_tables(positions)
    ones = jnp.ones((bsz, s_len, QK_NOPE), F32)
    zeros = jnp.zeros((bsz, s_len, HEAD_SLAB - QK_NOPE - QK_ROPE), F32)
    cosp = jnp.concatenate([ones, cos, cos, zeros], axis=-1)
    sinp = jnp.concatenate([jnp.zeros_like(ones), sin, sin, zeros], axis=-1)
    row = lambda a: a.reshape(1, -1)

    for l in range(depth):
        mod = mod_all[l]
        w_a, w_c, wuqa, wuqb, wuk, wuv, wvm, gb = _layer_weights(w_in[l], w_uq[l], w_ukv[l],
                                                                mlstm_gate_b[l])
        q, k, v, qm, km, vm, gates = _in_proj(
            x, mod, row(norm_mix_w[l]), w_a, row(q_norm_w[l]), row(kv_norm_w[l]), wuqa, wuqb,
            wuk, wuv, wvm, mlstm_conv_w[l], row(mlstm_conv_b[l]), gb, cosp, sinp, tm=tm, tq=tq, tl=tl)
        y_mla = _attention(q, k, v, tq=tq)
        hm = _mlstm(qm, km, vm, gates, row(mlstm_head_norm_w[l]), tl=tl, nc=nc)
        x = _mix_out(x, mod, row(norm_mix_w[l]), w_c, y_mla, hm, w_br_mla[l].astype(BF16),
                     w_br_mlstm[l].astype(BF16), w_out[l].astype(BF16), tm=tm)
        x = _ffn(x, mod, row(norm_ffn_w[l]), ffn_w_up[l].astype(BF16), ffn_conv_w[l],
                 row(ffn_conv_b[l]), ffn_w_down[l].astype(BF16), row(final_norm_w), tm=tf,
                 final_norm=(l == depth - 1))
    return x
```
